```python
import math
import jax, jax.numpy as jnp
from jax import lax
import numpy as np

D_MODEL = 1024
BATCH = 2
SEQ = 8192
DEPTH = 4
DEC_BATCH = 128
DEC_SEQ = 8
PAST_LEN = 8192
PAGE_SIZE = 128

N_A = DEPTH // 2
N_B = DEPTH - N_A
HEAD_DIM = 64
A_HEADS = D_MODEL // (2 * HEAD_DIM)
A_KV = A_HEADS // 2
A_GROUP = A_HEADS // A_KV
A_VDIM = 2 * HEAD_DIM
B_HEADS = D_MODEL // HEAD_DIM
B_GROUP = 8
B_KV = B_HEADS // B_GROUP
WINDOW = 128
Q_BLOCK = 128
ROPE_THETA = 500000.0
D_FF = ((8 * D_MODEL // 3 + 255) // 256) * 256
PLE_DIM = 256
EPS = 1e-6

kernel_name = "yoco_diffattn_swa_sink_decoder_step"


def rms_norm(x, g):
    x32 = x.astype(jnp.float32)
    y = x32 * lax.rsqrt(jnp.mean(x32 * x32, axis=-1, keepdims=True) + EPS)
    return (y * g.astype(jnp.float32)).astype(x.dtype)


def rope_partial(x, pos):
    rot = x.shape[-1] // 4
    half = rot // 2
    inv_freq = ROPE_THETA ** (-jnp.arange(half, dtype=jnp.float32) / half)
    ang = pos.astype(jnp.float32)[:, None] * inv_freq[None, :]
    shape = (1, x.shape[1]) + (1,) * (x.ndim - 3) + (half,)
    cos = jnp.cos(ang).reshape(shape).astype(x.dtype)
    sin = jnp.sin(ang).reshape(shape).astype(x.dtype)
    x1, x2 = x[..., :half], x[..., half:rot]
    return jnp.concatenate([x1 * cos - x2 * sin, x2 * cos + x1 * sin, x[..., rot:]], axis=-1)


def swiglu_half(x, g, w_in, w_out):
    h = rms_norm(x, g) @ w_in
    a, b = jnp.split(h, 2, axis=-1)
    return 0.5 * ((jax.nn.silu(a) * b) @ w_out)


def per_layer_embed(x, p, w_up, g_post, g_gate, w_gate):
    e = rms_norm(p @ w_up, g_post)
    gate = jax.nn.sigmoid(rms_norm(x, g_gate) @ w_gate)
    return x + gate * e


def diff_lambda(lq1, lk1, lq2, lk2, lam_init):
    f32 = jnp.float32
    return (jnp.exp(jnp.sum(lq1.astype(f32) * lk1.astype(f32)))
            - jnp.exp(jnp.sum(lq2.astype(f32) * lk2.astype(f32))) + lam_init)


def a_project(h, w_qkv, g_q, g_k, pos):
    b, t, _ = h.shape
    qw = A_HEADS * 2 * HEAD_DIM
    kw = A_KV * 2 * HEAD_DIM
    qkv = h @ w_qkv
    q = qkv[..., :qw].reshape(b, t, A_KV, A_GROUP, 2, HEAD_DIM)
    k = qkv[..., qw:qw + kw].reshape(b, t, A_KV, 2, HEAD_DIM)
    v = qkv[..., qw + kw:].reshape(b, t, A_KV, A_VDIM)
    q = rope_partial(rms_norm(q, g_q), pos)
    k = rope_partial(rms_norm(k, g_k), pos)
    return q, k, v


def a_core(q, k, v, q_pos, k_pos, lam):
    s = jnp.einsum('bqhgcd,bkhcd->bhgcqk', q, k).astype(jnp.float32) * (HEAD_DIM ** -0.5)
    mask = k_pos[None, :] <= q_pos[:, None]
    a = jax.nn.softmax(jnp.where(mask, s, -jnp.inf), axis=-1)
    w = (a[:, :, :, 0] - lam * a[:, :, :, 1]).astype(v.dtype)
    return jnp.einsum('bhgqk,bkhe->bqhge', w, v)


def a_prompt(q, k, v, lam):
    b, t = q.shape[:2]
    nb = t // Q_BLOCK
    qb = jnp.swapaxes(q.reshape((b, nb, Q_BLOCK) + q.shape[2:]), 0, 1)
    qpos = jnp.arange(t, dtype=jnp.int32).reshape(nb, Q_BLOCK)
    kpos = jnp.arange(t, dtype=jnp.int32)
    o = lax.map(lambda xs: a_core(xs[0], k, v, xs[1], kpos, lam), (qb, qpos))
    return jnp.swapaxes(o, 0, 1).reshape((b, t) + o.shape[3:])


def a_out(o, g_sub, lam_init, w_o):
    b, t = o.shape[:2]
    o = rms_norm(o, g_sub) * (1.0 - lam_init)
    return o.reshape(b, t, A_HEADS * A_VDIM) @ w_o


def shared_kv(h, g_kv, w_kv, g_k, pos):
    b, t, _ = h.shape
    k, v = jnp.split(rms_norm(h, g_kv) @ w_kv, 2, axis=-1)
    k = rope_partial(rms_norm(k.reshape(b, t, B_KV, HEAD_DIM), g_k), pos)
    return k, v.reshape(b, t, B_KV, HEAD_DIM)


def b_query(h, w_q, g_q, pos):
    b, t, _ = h.shape
    q = (h @ w_q).reshape(b, t, B_KV, B_GROUP, HEAD_DIM)
    return rope_partial(rms_norm(q, g_q), pos)


def b_core(q, k, v, q_pos, k_pos, sink):
    s = jnp.einsum('...qhgd,...khd->...hgqk', q, k).astype(jnp.float32) * (HEAD_DIM ** -0.5)
    d = q_pos[..., :, None] - k_pos[..., None, :]
    mask = (d >= 0) & (d < WINDOW) & (k_pos[..., None, :] >= 0)
    s = jnp.where(mask[..., None, None, :, :], s, -jnp.inf)
    sk = jnp.broadcast_to(sink.astype(jnp.float32).reshape(B_KV, B_GROUP, 1, 1), s.shape[:-1] + (1,))
    a = jax.nn.softmax(jnp.concatenate([s, sk], axis=-1), axis=-1)[..., :-1].astype(v.dtype)
    return jnp.einsum('...hgqk,...khd->...qhgd', a, v)


def b_prompt(q, k, v, sink):
    b, t = q.shape[:2]
    nb = t // WINDOW

    def band(z):
        zp = jnp.concatenate([jnp.zeros_like(z[:, :WINDOW]), z], axis=1)
        prev = zp[:, :t].reshape((b, nb, WINDOW) + z.shape[2:])
        cur = z.reshape((b, nb, WINDOW) + z.shape[2:])
        return jnp.concatenate([prev, cur], axis=2)

    qb = q.reshape((b, nb, WINDOW) + q.shape[2:])
    qpos = jnp.arange(t, dtype=jnp.int32).reshape(nb, WINDOW)
    kpos = qpos[:, :1] - WINDOW + jnp.arange(2 * WINDOW, dtype=jnp.int32)[None, :]
    o = b_core(qb, band(k), band(v), qpos, kpos, sink)
    return o.reshape((b, t) + o.shape[3:])


def b_out(o, w_o):
    b, t = o.shape[:2]
    return o.reshape(b, t, B_HEADS * HEAD_DIM) @ w_o


def setup_inputs(seed: int = 0) -> dict:
    key = jax.random.key(seed)
    ks = iter(jax.random.split(key, 64))
    f32 = jnp.float32

    def nrm(shape, scale):
        return scale * jax.random.normal(next(ks), shape, f32)

    def gain(shape):
        return 1.0 + 0.1 * jax.random.normal(next(ks), shape, f32)

    n_pages = PAST_LEN // PAGE_SIZE
    n_used = DEC_BATCH * n_pages
    n_pool = n_used + n_used // 4
    w_buf = min(WINDOW, PAST_LEN)
    qkv_w = (A_HEADS + 2 * A_KV) * 2 * HEAD_DIM
    d = D_MODEL
    page_table = jax.random.permutation(next(ks), n_pool)[:n_used].reshape(DEC_BATCH, n_pages).astype(jnp.int32)
    return {
        "x_prompt": nrm((BATCH, SEQ, d), 1.0),
        "x_sample": nrm((DEC_BATCH, DEC_SEQ, d), 1.0),
        "cache_k_a": nrm((N_A, n_pool, PAGE_SIZE, A_KV, 2, HEAD_DIM), 1.0),
        "cache_v_a": nrm((N_A, n_pool, PAGE_SIZE, A_KV, A_VDIM), 1.0),
        "cache_k_win": nrm((DEC_BATCH, w_buf, B_KV, HEAD_DIM), 1.0),
        "cache_v_win": nrm((DEC_BATCH, w_buf, B_KV, HEAD_DIM), 1.0),
        "page_table": page_table,
        "p_prompt": nrm((DEPTH, BATCH, SEQ, PLE_DIM), 1.0),
        "p_sample": nrm((DEPTH, DEC_BATCH, DEC_SEQ, PLE_DIM), 1.0),
        "g_ffn1": gain((DEPTH, d)),
        "w_ffn1_in": nrm((DEPTH, d, 2 * D_FF), d ** -0.5),
        "w_ffn1_out": nrm((DEPTH, D_FF, d), D_FF ** -0.5),
        "g_mix": gain((DEPTH, d)),
        "g_ffn2": gain((DEPTH, d)),
        "w_ffn2_in": nrm((DEPTH, d, 2 * D_FF), d ** -0.5),
        "w_ffn2_out": nrm((DEPTH, D_FF, d), D_FF ** -0.5),
        "w_qkv_a": nrm((N_A, d, qkv_w), d ** -0.5),
        "g_q_a": gain((N_A, HEAD_DIM)),
        "g_k_a": gain((N_A, HEAD_DIM)),
        "lambda_q1": nrm((N_A, HEAD_DIM), 0.1),
        "lambda_k1": nrm((N_A, HEAD_DIM), 0.1),
        "lambda_q2": nrm((N_A, HEAD_DIM), 0.1),
        "lambda_k2": nrm((N_A, HEAD_DIM), 0.1),
        "g_sub_a": gain((N_A, A_VDIM)),
        "w_o_a": nrm((N_A, A_HEADS * A_VDIM, d), (A_HEADS * A_VDIM) ** -0.5),
        "g_kv": gain((d,)),
        "w_kv": nrm((d, 2 * B_KV * HEAD_DIM), d ** -0.5),
        "g_k_b": gain((HEAD_DIM,)),
        "w_q_b": nrm((N_B, d, B_HEADS * HEAD_DIM), d ** -0.5),
        "g_q_b": gain((N_B, HEAD_DIM)),
        "sinks_b": nrm((N_B, B_HEADS), 1.0),
        "w_o_b": nrm((N_B, B_HEADS * HEAD_DIM, d), (B_HEADS * HEAD_DIM) ** -0.5),
        "w_ple_up": nrm((DEPTH, PLE_DIM, d), PLE_DIM ** -0.5),
        "g_ple_post": gain((DEPTH, d)),
        "g_ple_gate": gain((DEPTH, d)),
        "w_ple_gate": nrm((DEPTH, d, d), d ** -0.5),
    }


def reference(x_prompt, x_sample, cache_k_a, cache_v_a, cache_k_win, cache_v_win, page_table,
              p_prompt, p_sample, g_ffn1, w_ffn1_in, w_ffn1_out, g_mix, g_ffn2, w_ffn2_in, w_ffn2_out,
              w_qkv_a, g_q_a, g_k_a, lambda_q1, lambda_k1, lambda_q2, lambda_k2, g_sub_a, w_o_a,
              g_kv, w_kv, g_k_b, w_q_b, g_q_b, sinks_b, w_o_b,
              w_ple_up, g_ple_post, g_ple_gate, w_ple_gate):
    n_pages = PAST_LEN // PAGE_SIZE
    w_buf = min(WINDOW, PAST_LEN)
    dec_b = x_sample.shape[0]
    pos_p = jnp.arange(SEQ, dtype=jnp.int32)
    pos_s = PAST_LEN + jnp.arange(DEC_SEQ, dtype=jnp.int32)
    pos_past = jnp.arange(PAST_LEN + DEC_SEQ, dtype=jnp.int32)
    pos_win = PAST_LEN - w_buf + jnp.arange(w_buf + DEC_SEQ, dtype=jnp.int32)

    xp, xs = x_prompt, x_sample
    ka_p, va_p, ka_s, va_s = [], [], [], []
    for i in range(DEPTH):
        if i == N_A:
            kb_p, vb_p = shared_kv(xp, g_kv, w_kv, g_k_b, pos_p)
            kb_s, vb_s = shared_kv(xs, g_kv, w_kv, g_k_b, pos_s)
            kw_s = jnp.concatenate([cache_k_win, kb_s], axis=1)
            vw_s = jnp.concatenate([cache_v_win, vb_s], axis=1)

        xp = xp + swiglu_half(xp, g_ffn1[i], w_ffn1_in[i], w_ffn1_out[i])
        xs = xs + swiglu_half(xs, g_ffn1[i], w_ffn1_in[i], w_ffn1_out[i])

        if i < N_A:
            lam_init = 0.8 - 0.6 * math.exp(-0.3 * i)
            lam = diff_lambda(lambda_q1[i], lambda_k1[i], lambda_q2[i], lambda_k2[i], lam_init)
            q, k, v = a_project(rms_norm(xp, g_mix[i]), w_qkv_a[i], g_q_a[i], g_k_a[i], pos_p)
            ka_p.append(k)
            va_p.append(v)
            xp = xp + a_out(a_prompt(q, k, v, lam), g_sub_a[i], lam_init, w_o_a[i])

            q, k, v = a_project(rms_norm(xs, g_mix[i]), w_qkv_a[i], g_q_a[i], g_k_a[i], pos_s)
            ka_s.append(k)
            va_s.append(v)
            k_past = cache_k_a[i][page_table].reshape((dec_b, n_pages * PAGE_SIZE) + k.shape[2:])
            v_past = cache_v_a[i][page_table].reshape((dec_b, n_pages * PAGE_SIZE) + v.shape[2:])
            k_all = jnp.concatenate([k_past, k], axis=1)
            v_all = jnp.concatenate([v_past, v], axis=1)
            xs = xs + a_out(a_core(q, k_all, v_all, pos_s, pos_past, lam), g_sub_a[i], lam_init, w_o_a[i])
        else:
            j = i - N_A
            qp = b_query(rms_norm(xp, g_mix[i]), w_q_b[j], g_q_b[j], pos_p)
            xp = xp + b_out(b_prompt(qp, kb_p, vb_p, sinks_b[j]), w_o_b[j])
            qs = b_query(rms_norm(xs, g_mix[i]), w_q_b[j], g_q_b[j], pos_s)
            xs = xs + b_out(b_core(qs, kw_s, vw_s, pos_s, pos_win, sinks_b[j]), w_o_b[j])

        xp = xp + swiglu_half(xp, g_ffn2[i], w_ffn2_in[i], w_ffn2_out[i])
        xs = xs + swiglu_half(xs, g_ffn2[i], w_ffn2_in[i], w_ffn2_out[i])
        xp = per_layer_embed(xp, p_prompt[i], w_ple_up[i], g_ple_post[i], g_ple_gate[i], w_ple_gate[i])
        xs = per_layer_embed(xs, p_sample[i], w_ple_up[i], g_ple_post[i], g_ple_gate[i], w_ple_gate[i])

    return (xp, xs, jnp.stack(ka_p), jnp.stack(va_p), jnp.stack(ka_s), jnp.stack(va_s),
            kb_p[:, -w_buf:], vb_p[:, -w_buf:], kw_s[:, -w_buf:], vw_s[:, -w_buf:])
```

```python
import functools
import math

import jax
import jax.numpy as jnp
from jax import lax
from jax.experimental import pallas as pl
from jax.experimental.pallas import tpu as pltpu

F32 = jnp.float32
BF16 = jnp.bfloat16

HEAD_DIM = 64
WINDOW = 128
ROPE_THETA = 500000.0
EPS = 1e-6
LANES = 128
QK_SCALE = HEAD_DIM ** -0.5
VMEM_LIMIT_BYTES = 56 * 1024 * 1024
NEG_INF = float("-inf")


def _params(n_axes):
    return pltpu.CompilerParams(
        dimension_semantics=("arbitrary",) * n_axes,
        vmem_limit_bytes=VMEM_LIMIT_BYTES)


def _resident(shape):
    nd = len(shape)
    return pl.BlockSpec(shape, lambda *_: (0,) * nd, pipeline_mode=pl.Buffered(1))


def _rms(x, g):
    ms = jnp.mean(x * x, axis=-1, keepdims=True)
    return x * lax.rsqrt(ms + EPS) * g


def _dot(a, b):
    return jnp.dot(a, b, preferred_element_type=F32)


def _dot_nt(a, b):
    return lax.dot_general(a, b, (((1,), (1,)), ((), ())), preferred_element_type=F32)


def _ffn_kernel(x_ref, g_ref, win_ref, wout_ref, o_ref, h_scr, *, d_ff, chunk):
    x = x_ref[...]
    xn = _rms(x, g_ref[...]).astype(BF16)
    for c in range(d_ff // chunk):
        a = _dot(xn, win_ref[:, c * chunk:(c + 1) * chunk])
        b = _dot(xn, win_ref[:, d_ff + c * chunk:d_ff + (c + 1) * chunk])
        h_scr[:, c * chunk:(c + 1) * chunk] = (a * jax.nn.sigmoid(a) * b).astype(BF16)
    o_ref[...] = x + 0.5 * _dot(h_scr[...], wout_ref[...])


def _ffn_half(x, g, w_in, w_out, tm):
    n, d = x.shape
    d_ff = w_out.shape[0]
    chunk = 256
    assert d_ff % chunk == 0 and n % tm == 0
    return pl.pallas_call(
        functools.partial(_ffn_kernel, d_ff=d_ff, chunk=chunk),
        grid=(n // tm,),
        in_specs=[pl.BlockSpec((tm, d), lambda i: (i, 0)),
                  _resident((1, d)), _resident(w_in.shape), _resident(w_out.shape)],
        out_specs=pl.BlockSpec((tm, d), lambda i: (i, 0)),
        out_shape=jax.ShapeDtypeStruct((n, d), F32),
        scratch_shapes=[pltpu.VMEM((tm, d_ff), BF16)],
        compiler_params=_params(1),
        name="ffn_half",
    )(x, g.reshape(1, d), w_in, w_out)


def _ple_kernel(x_ref, p_ref, wup_ref, gpost_ref, ggate_ref, wgate_ref, o_ref):
    x = x_ref[...]
    e = _rms(_dot(p_ref[...].astype(BF16), wup_ref[...]), gpost_ref[...])
    gate = jax.nn.sigmoid(_dot(_rms(x, ggate_ref[...]).astype(BF16), wgate_ref[...]))
    o_ref[...] = x + gate * e


def _ple(x, p, w_up, g_post, g_gate, w_gate, tm):
    n, d = x.shape
    pd = p.shape[1]
    return pl.pallas_call(
        _ple_kernel,
        grid=(n // tm,),
        in_specs=[pl.BlockSpec((tm, d), lambda i: (i, 0)),
                  pl.BlockSpec((tm, pd), lambda i: (i, 0)),
                  _resident(w_up.shape), _resident((1, d)), _resident((1, d)),
                  _resident(w_gate.shape)],
        out_specs=pl.BlockSpec((tm, d), lambda i: (i, 0)),
        out_shape=jax.ShapeDtypeStruct((n, d), F32),
        compiler_params=_params(1),
        name="per_layer_embed",
    )(x, p, w_up, g_post.reshape(1, d), g_gate.reshape(1, d), w_gate)


def _linres_kernel(x_ref, o_ref, w_ref, y_ref):
    y_ref[...] = x_ref[...] + _dot(o_ref[...].astype(BF16), w_ref[...])


def _linear_residual(x, o, w, tm):
    n, d = x.shape
    return pl.pallas_call(
        _linres_kernel,
        grid=(n // tm,),
        in_specs=[pl.BlockSpec((tm, d), lambda i: (i, 0)),
                  pl.BlockSpec((tm, o.shape[1]), lambda i: (i, 0)),
                  _resident(w.shape)],
        out_specs=pl.BlockSpec((tm, d), lambda i: (i, 0)),
        out_shape=jax.ShapeDtypeStruct((n, d), F32),
        compiler_params=_params(1),
        name="linear_residual",
    )(x, o, w)


def _headnorm_rope(x, g2, cos, sin_a, sin_b, ones_blk):
    xx = x * x
    hi = xx.astype(BF16)
    lo = (xx - hi.astype(F32)).astype(BF16)
    ss = _dot(hi, ones_blk) + _dot(lo, ones_blk)
    y = x * lax.rsqrt(ss * (1.0 / HEAD_DIM) + EPS) * g2
    half = HEAD_DIM // 8
    return (y * cos + pltpu.roll(y, LANES - half, 1) * sin_a
            + pltpu.roll(y, half, 1) * sin_b)


def _proj_kernel(*refs, segments, n_gains):
    x_ref, g_ref, w_ref = refs[:3]
    gain_refs = refs[3:3 + n_gains]
    cos_ref, sa_ref, sb_ref, ones_ref = refs[3 + n_gains:7 + n_gains]
    out_refs = refs[7 + n_gains:]
    xn = _rms(x_ref[...], g_ref[...]).astype(BF16)
    y = _dot(xn, w_ref[...])
    cos, sin_a, sin_b, ones_blk = cos_ref[...], sa_ref[...], sb_ref[...], ones_ref[...]
    col = 0
    for o_ref, (n_chunks, gain_idx, scale) in zip(out_refs, segments):
        for j in range(n_chunks):
            c = y[:, col:col + LANES]
            if gain_idx is not None:
                c = _headnorm_rope(c, gain_refs[gain_idx][...], cos, sin_a, sin_b, ones_blk)
                if scale != 1.0:
                    c = c * scale
            o_ref[:, j * LANES:(j + 1) * LANES] = c
            col += LANES


def _project(x, g, w, gains, segments, rope, tm, name):
    n, d = x.shape
    cos_t, sa_t, sb_t, tab_map = rope
    ones_blk = (jnp.arange(LANES)[:, None] // HEAD_DIM
                == jnp.arange(LANES)[None, :] // HEAD_DIM).astype(BF16)
    gains2 = [jnp.tile(gv.reshape(1, HEAD_DIM), (1, LANES // HEAD_DIM)) for gv in gains]
    tab_spec = pl.BlockSpec((tm, LANES), tab_map)
    assert sum(s[0] for s in segments) * LANES == w.shape[1]
    return pl.pallas_call(
        functools.partial(_proj_kernel, segments=tuple(segments), n_gains=len(gains)),
        grid=(n // tm,),
        in_specs=[pl.BlockSpec((tm, d), lambda i: (i, 0)), _resident((1, d)),
                  _resident(w.shape)]
                 + [_resident((1, LANES))] * len(gains)
                 + [tab_spec, tab_spec, tab_spec, _resident((LANES, LANES))],
        out_specs=[pl.BlockSpec((tm, s[0] * LANES), lambda i: (i, 0)) for s in segments],
        out_shape=[jax.ShapeDtypeStruct((n, s[0] * LANES), F32) for s in segments],
        compiler_params=_params(1),
        name=name,
    )(x, g.reshape(1, d), w, *gains2, cos_t, sa_t, sb_t, ones_blk)


def _rope_tables(seq, n_sample_pos, past_len, tm):
    half = HEAD_DIM // 8
    inv_freq = ROPE_THETA ** (-jnp.arange(half, dtype=F32) / half)
    pos = jnp.concatenate([
        jnp.arange(seq, dtype=jnp.int32),
        past_len + (jnp.arange(tm, dtype=jnp.int32) % n_sample_pos)])
    ang = pos.astype(F32)[:, None] * inv_freq[None, :]
    cos, sin = jnp.cos(ang), jnp.sin(ang)
    rows = pos.shape[0]
    pad = jnp.zeros((rows, HEAD_DIM - 2 * half), F32)
    zero = jnp.zeros((rows, half), F32)
    reps = (1, LANES // HEAD_DIM)
    cos_t = jnp.tile(jnp.concatenate([cos, cos, pad + 1.0], axis=1), reps)
    sin_a = jnp.tile(jnp.concatenate([-sin, zero, pad], axis=1), reps)
    sin_b = jnp.tile(jnp.concatenate([zero, sin, pad], axis=1), reps)
    return cos_t, sin_a, sin_b


def _flash_update(s, m_scr, l_scr):
    m_prev = m_scr[...]
    m_next = jnp.maximum(m_prev, jnp.max(s, axis=1, keepdims=True))
    p = jnp.exp(s - jnp.tile(m_next, (1, s.shape[1] // LANES)))
    alpha = jnp.exp(m_prev - m_next)
    l_scr[...] = alpha * l_scr[...] + jnp.sum(p, axis=1, keepdims=True)
    m_scr[...] = m_next
    return p, alpha


def _diff_lambda(lq1_ref, lk1_ref, lq2_ref, lk2_ref, lam_init):
    return (jnp.exp(jnp.sum(lq1_ref[...] * lk1_ref[...], axis=1, keepdims=True))
            - jnp.exp(jnp.sum(lq2_ref[...] * lk2_ref[...], axis=1, keepdims=True))
            + lam_init)


def _sub_norm(o1, o2, lam, g_sub, lam_init):
    d = o1 - lam * o2
    return _rms(d, g_sub) * (1.0 - lam_init)


def _aprompt_kernel(q_ref, k_ref, v_ref, lq1_ref, lk1_ref, lq2_ref, lk2_ref, gsub_ref,
                    o_ref, q4_scr, m_scr, l_scr, acc_scr, *, blk, lam_init):
    qi = pl.program_id(2)
    lane = lax.broadcasted_iota(jnp.int32, (blk, LANES), 1)
    first = lane < HEAD_DIM
    for g in range(2):
        qg = q_ref[:, g * LANES:(g + 1) * LANES]
        q4_scr[g * blk:(g + 1) * blk, :] = jnp.where(first, qg, 0.0).astype(BF16)
        q4_scr[(2 + g) * blk:(3 + g) * blk, :] = jnp.where(first, 0.0, qg).astype(BF16)
    m_scr[...] = jnp.full(m_scr.shape, NEG_INF, F32)
    l_scr[...] = jnp.zeros(l_scr.shape, F32)
    acc_scr[...] = jnp.zeros(acc_scr.shape, F32)

    def step(start, masked):
        kb = k_ref[pl.ds(start, blk), :].astype(BF16)
        vb = v_ref[pl.ds(start, blk), :].astype(BF16)
        s = _dot_nt(q4_scr[...], kb)
        if masked:
            tok = jnp.bitwise_and(lax.broadcasted_iota(jnp.int32, s.shape, 0), blk - 1)
            key = lax.broadcasted_iota(jnp.int32, s.shape, 1)
            s = jnp.where(key <= tok, s, NEG_INF)
        p, alpha = _flash_update(s, m_scr, l_scr)
        acc_scr[...] = alpha * acc_scr[...] + _dot(p.astype(BF16), vb)

    def body(i, carry):
        step(pl.multiple_of(i * blk, blk), False)
        return carry

    lax.fori_loop(0, qi, body, 0)
    step(pl.multiple_of(qi * blk, blk), True)

    o = acc_scr[...] / l_scr[...]
    lam = _diff_lambda(lq1_ref, lk1_ref, lq2_ref, lk2_ref, lam_init)
    for g in range(2):
        o1 = o[g * blk:(g + 1) * blk, :]
        o2 = o[(2 + g) * blk:(3 + g) * blk, :]
        o_ref[:, g * LANES:(g + 1) * LANES] = _sub_norm(o1, o2, lam, gsub_ref[...], lam_init)


def _a_prompt_attention(q, k, v, lam_vecs, g_sub, lam_init, batch, seq, blk):
    n_kv = k.shape[1] // LANES
    nq = seq // blk
    vec = lambda a: a.reshape(1, -1)
    return pl.pallas_call(
        functools.partial(_aprompt_kernel, blk=blk, lam_init=lam_init),
        grid=(batch, n_kv, nq),
        in_specs=[pl.BlockSpec((blk, 2 * LANES), lambda b, h, i: (b * nq + i, h)),
                  pl.BlockSpec((seq, LANES), lambda b, h, i: (b, h)),
                  pl.BlockSpec((seq, LANES), lambda b, h, i: (b, h))]
                 + [_resident((1, HEAD_DIM))] * 4 + [_resident((1, LANES))],
        out_specs=pl.BlockSpec((blk, 2 * LANES), lambda b, h, i: (b * nq + i, h)),
        out_shape=jax.ShapeDtypeStruct((batch * seq, 2 * n_kv * LANES), F32),
        scratch_shapes=[pltpu.VMEM((4 * blk, LANES), BF16),
                        pltpu.VMEM((4 * blk, LANES), F32),
                        pltpu.VMEM((4 * blk, LANES), F32),
                        pltpu.VMEM((4 * blk, LANES), F32)],
        compiler_params=_params(3),
        name="a_prompt_attention",
    )(q, k, v, *[vec(a) for a in lam_vecs], vec(g_sub))


def _asample_kernel(pt_ref, q_ref, kn_ref, vn_ref, lq1_ref, lk1_ref, lq2_ref, lk2_ref,
                    gsub_ref, *rest, n_pages_step, n_kv, page, lam_init):
    del pt_ref
    k_pages = rest[:n_pages_step]
    v_pages = rest[n_pages_step:2 * n_pages_step]
    o_ref, qbd_scr, kbuf, vbuf, m_scr, l_scr, acc_scr = rest[2 * n_pages_step:]
    j = pl.program_id(1)
    t = q_ref.shape[0]
    rows_kv = 4 * t

    @pl.when(j == 0)
    def _():
        qbd_scr[...] = jnp.zeros(qbd_scr.shape, F32)
        lane = lax.broadcasted_iota(jnp.int32, (t, LANES), 1)
        first = lane < HEAD_DIM
        for h in range(n_kv):
            for c in range(2):
                for g in range(2):
                    r0 = ((h * 2 + c) * 2 + g) * t
                    qg = q_ref[:, (2 * h + g) * LANES:(2 * h + g + 1) * LANES]
                    keep = first if c == 0 else jnp.logical_not(first)
                    qbd_scr[r0:r0 + t, h * LANES:(h + 1) * LANES] = jnp.where(keep, qg, 0.0)
        m_scr[...] = jnp.full(m_scr.shape, NEG_INF, F32)
        l_scr[...] = jnp.zeros(l_scr.shape, F32)
        acc_scr[...] = jnp.zeros(acc_scr.shape, F32)

    def attend(kb, vb, mask_new):
        s = _dot_nt(qbd_scr[...].astype(BF16), kb)
        if mask_new:
            tok = jnp.bitwise_and(lax.broadcasted_iota(jnp.int32, s.shape, 0), t - 1)
            key = lax.broadcasted_iota(jnp.int32, s.shape, 1)
            s = jnp.where(key <= tok, s, NEG_INF)
        p, alpha = _flash_update(s, m_scr, l_scr)
        pb = p.astype(BF16)
        for h in range(n_kv):
            r = slice(h * rows_kv, (h + 1) * rows_kv)
            acc_scr[r, :] = alpha[r, :] * acc_scr[r, :] + _dot(
                pb[r, :], vb[:, h * LANES:(h + 1) * LANES])

    for r in range(n_pages_step):
        kbuf[r * page:(r + 1) * page, :] = k_pages[r][...].astype(BF16)
        vbuf[r * page:(r + 1) * page, :] = v_pages[r][...].astype(BF16)
    attend(kbuf[...], vbuf[...], False)

    @pl.when(j == pl.num_programs(1) - 1)
    def _():
        pad = jnp.zeros((LANES - t, kn_ref.shape[1]), F32)
        kn = jnp.concatenate([kn_ref[...], pad], axis=0).astype(BF16)
        vn = jnp.concatenate([vn_ref[...], pad], axis=0).astype(BF16)
        attend(kn, vn, True)
        o = acc_scr[...] / l_scr[...]
        lam = _diff_lambda(lq1_ref, lk1_ref, lq2_ref, lk2_ref, lam_init)
        for h in range(n_kv):
            for g in range(2):
                r1 = ((h * 2 + 0) * 2 + g) * t
                r2 = ((h * 2 + 1) * 2 + g) * t
                o_ref[:, (2 * h + g) * LANES:(2 * h + g + 1) * LANES] = _sub_norm(
                    o[r1:r1 + t, :], o[r2:r2 + t, :], lam, gsub_ref[...], lam_init)


def _a_sample_attention(q, k, v, cache_k, cache_v, layer, page_table, lam_vecs, g_sub,
                        lam_init, n_prompt, dec_b, dec_t):
    n_pages = page_table.shape[1]
    page = cache_k.shape[2]
    kw = cache_k.shape[3]
    n_kv = kw // LANES
    assert 4 * dec_t * n_kv == LANES and page == LANES
    n_pages_step = math.gcd(n_pages, 8)
    row0 = n_prompt // dec_t
    vec = lambda a: a.reshape(1, -1)

    def page_spec(r):
        return pl.BlockSpec(
            (None, None, page, kw),
            lambda b, j, pt: (layer, pt[b, j * n_pages_step + r], 0, 0))

    tok_map = lambda b, j, pt: (row0 + b, 0)
    grid_spec = pltpu.PrefetchScalarGridSpec(
        num_scalar_prefetch=1,
        grid=(dec_b, n_pages // n_pages_step),
        in_specs=[pl.BlockSpec((dec_t, q.shape[1]), tok_map),
                  pl.BlockSpec((dec_t, kw), tok_map),
                  pl.BlockSpec((dec_t, kw), tok_map)]
                 + [pl.BlockSpec((1, HEAD_DIM), lambda b, j, pt: (0, 0))] * 4
                 + [pl.BlockSpec((1, LANES), lambda b, j, pt: (0, 0))]
                 + [page_spec(r) for r in range(n_pages_step)] * 2,
        out_specs=pl.BlockSpec((dec_t, q.shape[1]), lambda b, j, pt: (b, 0)),
        scratch_shapes=[pltpu.VMEM((LANES, kw), F32),
                        pltpu.VMEM((n_pages_step * page, kw), BF16),
                        pltpu.VMEM((n_pages_step * page, kw), BF16),
                        pltpu.VMEM((LANES, LANES), F32),
                        pltpu.VMEM((LANES, LANES), F32),
                        pltpu.VMEM((LANES, LANES), F32)])
    return pl.pallas_call(
        functools.partial(_asample_kernel, n_pages_step=n_pages_step, n_kv=n_kv, page=page,
                          lam_init=lam_init),
        grid_spec=grid_spec,
        out_shape=jax.ShapeDtypeStruct((dec_b * dec_t, q.shape[1]), F32),
        compiler_params=_params(2),
        name="a_sample_attention",
    )(page_table, q, k, v, *[vec(a) for a in lam_vecs], vec(g_sub),
      *([cache_k] * n_pages_step), *([cache_v] * n_pages_step))


def _dup_head(x, h):
    lane = lax.broadcasted_iota(jnp.int32, x.shape, 1)
    keep = (lane < HEAD_DIM) if h == 0 else (lane >= HEAD_DIM)
    return jnp.where(keep, x, pltpu.roll(x, HEAD_DIM, 1))


def _swa_heads(q_ref, k2, v2, sink_ref, o_ref, valid, t, n_group):
    lane = lax.broadcasted_iota(jnp.int32, (t, LANES), 1)
    first = lane < HEAD_DIM
    n_kv = k2.shape[1] // HEAD_DIM
    pairs = n_group // 2
    for h in range(n_kv):
        kd = _dup_head(k2, h).astype(BF16)
        vd = _dup_head(v2, h).astype(BF16)
        parts = []
        for r in range(n_group):
            c = h * pairs + r // 2
            q2 = q_ref[:, c * LANES:(c + 1) * LANES]
            parts.append(jnp.where(first if r % 2 == 0 else jnp.logical_not(first), q2, 0.0))
        qs = jnp.concatenate(parts, axis=0).astype(BF16)
        s = _dot_nt(qs, kd)
        ps = []
        for r in range(n_group):
            sr = jnp.where(valid, s[r * t:(r + 1) * t, :], NEG_INF)
            sink = sink_ref[h * n_group + r]
            m = jnp.maximum(jnp.max(sr, axis=1, keepdims=True), sink)
            p = jnp.exp(sr - m)
            l = jnp.sum(p, axis=1, keepdims=True) + jnp.exp(sink - m)
            ps.append(p / l)
        o = _dot(jnp.concatenate(ps, axis=0).astype(BF16), vd)
        for pr in range(pairs):
            c = h * pairs + pr
            o_ref[:, c * LANES:(c + 1) * LANES] = jnp.where(
                first, o[(2 * pr) * t:(2 * pr + 1) * t, :], o[(2 * pr + 1) * t:(2 * pr + 2) * t, :])


def _bprompt_kernel(q_ref, kp_ref, kc_ref, vp_ref, vc_ref, sink_ref, o_ref, *, n_group):
    i = pl.program_id(1)
    t = q_ref.shape[0]
    k2 = jnp.concatenate([kp_ref[...], kc_ref[...]], axis=0)
    v2 = jnp.concatenate([vp_ref[...], vc_ref[...]], axis=0)
    tok = lax.broadcasted_iota(jnp.int32, (t, 2 * t), 0)
    key = lax.broadcasted_iota(jnp.int32, (t, 2 * t), 1)
    valid = (key > tok) & (key <= tok + t) & (key >= jnp.where(i > 0, 0, t))
    _swa_heads(q_ref, k2, v2, sink_ref, o_ref, valid, t, n_group)


def _b_prompt_attention(q, kb, vb, sinks, batch, seq):
    t = WINDOW
    nb = seq // t
    n_group = (q.shape[1] // HEAD_DIM) // (kb.shape[1] // HEAD_DIM)
    cur = lambda b, i: (b * nb + i, 0)
    prev = lambda b, i: (b * nb + jnp.maximum(i - 1, 0), 0)
    return pl.pallas_call(
        functools.partial(_bprompt_kernel, n_group=n_group),
        grid=(batch, nb),
        in_specs=[pl.BlockSpec((t, q.shape[1]), cur),
                  pl.BlockSpec((t, LANES), prev), pl.BlockSpec((t, LANES), cur),
                  pl.BlockSpec((t, LANES), prev), pl.BlockSpec((t, LANES), cur),
                  pl.BlockSpec(memory_space=pltpu.SMEM)],
        out_specs=pl.BlockSpec((t, q.shape[1]), cur),
        out_shape=jax.ShapeDtypeStruct((batch * seq, q.shape[1]), F32),
        compiler_params=_params(2),
        name="b_prompt_attention",
    )(q, kb, kb, vb, vb, sinks)


def _bsample_kernel(q_ref, kw_ref, kn_ref, vw_ref, vn_ref, sink_ref, o_ref, *, n_group):
    t = q_ref.shape[0]
    w = kw_ref.shape[0]
    pad = jnp.zeros((w - t, LANES), F32)
    k2 = jnp.concatenate([kw_ref[...], kn_ref[...], pad], axis=0)
    v2 = jnp.concatenate([vw_ref[...], vn_ref[...], pad], axis=0)
    tok = lax.broadcasted_iota(jnp.int32, (t, 2 * w), 0)
    key = lax.broadcasted_iota(jnp.int32, (t, 2 * w), 1)
    valid = (key > tok + (w - WINDOW)) & (key <= tok + w)
    _swa_heads(q_ref, k2, v2, sink_ref, o_ref, valid, t, n_group)


def _b_sample_attention(q, kb, vb, cache_k_win, cache_v_win, sinks, n_prompt, dec_b, dec_t):
    w = cache_k_win.shape[1]
    row0 = n_prompt // dec_t
    n_group = (q.shape[1] // HEAD_DIM) // (kb.shape[1] // HEAD_DIM)
    tok = lambda b: (row0 + b, 0)
    win = pl.BlockSpec((None, w, LANES), lambda b: (b, 0, 0))
    return pl.pallas_call(
        functools.partial(_bsample_kernel, n_group=n_group),
        grid=(dec_b,),
        in_specs=[pl.BlockSpec((dec_t, q.shape[1]), tok),
                  win, pl.BlockSpec((dec_t, LANES), tok),
                  win, pl.BlockSpec((dec_t, LANES), tok),
                  pl.BlockSpec(memory_space=pltpu.SMEM)],
        out_specs=pl.BlockSpec((dec_t, q.shape[1]), lambda b: (b, 0)),
        out_shape=jax.ShapeDtypeStruct((dec_b * dec_t, q.shape[1]), F32),
        compiler_params=_params(1),
        name="b_sample_attention",
    )(q, cache_k_win.reshape(dec_b, w, LANES), kb, cache_v_win.reshape(dec_b, w, LANES), vb,
      sinks)


def _token_tile(seq, n_sample):
    for tm in (512, 256, 128, 64, 32, 16, 8):
        if seq % tm == 0 and n_sample % tm == 0:
            return tm
    raise ValueError("unsupported token counts")


def kernel(x_prompt, x_sample, cache_k_a, cache_v_a, cache_k_win, cache_v_win, page_table, p_prompt, p_sample, g_ffn1, w_ffn1_in, w_ffn1_out, g_mix, g_ffn2, w_ffn2_in, w_ffn2_out, w_qkv_a, g_q_a, g_k_a, lambda_q1, lambda_k1, lambda_q2, lambda_k2, g_sub_a, w_o_a, g_kv, w_kv, g_k_b, w_q_b, g_q_b, sinks_b, w_o_b, w_ple_up, g_ple_post, g_ple_gate, w_ple_gate):
    batch, seq, d = x_prompt.shape
    dec_b, dec_t, _ = x_sample.shape
    depth = g_ffn1.shape[0]
    n_a = w_qkv_a.shape[0]
    n_pool, page = cache_k_a.shape[1], cache_k_a.shape[2]
    a_kv = cache_k_a.shape[3]
    past_len = page_table.shape[1] * page
    w_buf = cache_k_win.shape[1]
    n_prompt = batch * seq
    n_sample = dec_b * dec_t
    tm = _token_tile(seq, n_sample)
    assert tm % dec_t == 0 and seq % WINDOW == 0 and w_buf == WINDOW

    bf = lambda w: w.astype(BF16)
    x = jnp.concatenate([x_prompt.reshape(n_prompt, d), x_sample.reshape(n_sample, d)], axis=0)
    p_all = jnp.concatenate([p_prompt.reshape(depth, n_prompt, -1),
                             p_sample.reshape(depth, n_sample, -1)], axis=1)
    cache_k = cache_k_a.reshape(n_a, n_pool, page, -1)
    cache_v = cache_v_a.reshape(n_a, n_pool, page, -1)

    n_pt, spt = n_prompt // tm, seq // tm
    tab_map = lambda i: (jnp.where(i < n_pt, i % spt, spt), 0)
    rope = _rope_tables(seq, dec_t, past_len, tm) + (tab_map,)
    attn_blk = 256 if seq % 256 == 0 else WINDOW

    ka, va = [], []
    kb = vb = None
    for i in range(depth):
        if i == n_a:
            kb, vb = _project(x, g_kv, bf(w_kv), [g_k_b], [(1, 0, 1.0), (1, None, 1.0)],
                              rope, tm, "shared_kv_proj")
        x = _ffn_half(x, g_ffn1[i], bf(w_ffn1_in[i]), bf(w_ffn1_out[i]), tm)
        if i < n_a:
            lam_init = 0.8 - 0.6 * math.exp(-0.3 * i)
            lam_vecs = (lambda_q1[i], lambda_k1[i], lambda_q2[i], lambda_k2[i])
            q, k, v = _project(
                x, g_mix[i], bf(w_qkv_a[i]), [g_q_a[i], g_k_a[i]],
                [(2 * a_kv * 2 * HEAD_DIM // LANES, 0, QK_SCALE),
                 (a_kv * 2 * HEAD_DIM // LANES, 1, 1.0),
                 (a_kv * 2 * HEAD_DIM // LANES, None, 1.0)],
                rope, tm, "a_qkv_proj")
            ka.append(k)
            va.append(v)
            o_p = _a_prompt_attention(q, k, v, lam_vecs, g_sub_a[i], lam_init, batch, seq,
                                      attn_blk)
            o_s = _a_sample_attention(q, k, v, cache_k, cache_v, i, page_table, lam_vecs,
                                      g_sub_a[i], lam_init, n_prompt, dec_b, dec_t)
            x = _linear_residual(x, jnp.concatenate([o_p, o_s], axis=0), bf(w_o_a[i]), tm)
        else:
            j = i - n_a
            (q,) = _project(x, g_mix[i], bf(w_q_b[j]), [g_q_b[j]],
                            [(w_q_b.shape[2] // LANES, 0, QK_SCALE)], rope, tm, "b_q_proj")
            o_p = _b_prompt_attention(q, kb, vb, sinks_b[j], batch, seq)
            o_s = _b_sample_attention(q, kb, vb, cache_k_win, cache_v_win, sinks_b[j],
                                      n_prompt, dec_b, dec_t)
            x = _linear_residual(x, jnp.concatenate([o_p, o_s], axis=0), bf(w_o_b[j]), tm)
        x = _ffn_half(x, g_ffn2[i], bf(w_ffn2_in[i]), bf(w_ffn2_out[i]), tm)
        x = _ple(x, p_all[i], bf(w_ple_up[i]), g_ple_post[i], g_ple_gate[i],
                 bf(w_ple_gate[i]), tm)

    ka, va = jnp.stack(ka), jnp.stack(va)
    b_kv = kb.shape[1] // HEAD_DIM
    kb_p = kb[:n_prompt].reshape(batch, seq, b_kv, HEAD_DIM)
    vb_p = vb[:n_prompt].reshape(batch, seq, b_kv, HEAD_DIM)
    kb_s = kb[n_prompt:].reshape(dec_b, dec_t, b_kv, HEAD_DIM)
    vb_s = vb[n_prompt:].reshape(dec_b, dec_t, b_kv, HEAD_DIM)
    kw_s = jnp.concatenate([cache_k_win, kb_s], axis=1)
    vw_s = jnp.concatenate([cache_v_win, vb_s], axis=1)
    return (x[:n_prompt].reshape(batch, seq, d),
            x[n_prompt:].reshape(dec_b, dec_t, d),
            ka[:, :n_prompt].reshape(n_a, batch, seq, a_kv, 2, HEAD_DIM),
            va[:, :n_prompt].reshape(n_a, batch, seq, a_kv, 2 * HEAD_DIM),
            ka[:, n_prompt:].reshape(n_a, dec_b, dec_t, a_kv, 2, HEAD_DIM),
            va[:, n_prompt:].reshape(n_a, dec_b, dec_t, a_kv, 2 * HEAD_DIM),
            kb_p[:, -w_buf:], vb_p[:, -w_buf:], kw_s[:, -w_buf:], vw_s[:, -w_buf:])
```

```python
import functools
import math

import jax
import jax.numpy as jnp
from jax import lax
from jax.experimental import pallas as pl
from jax.experimental.pallas import tpu as pltpu

F32 = jnp.float32
BF16 = jnp.bfloat16

HEAD_DIM = 64
WINDOW = 128
ROPE_THETA = 500000.0
EPS = 1e-6
LANES = 128
QK_SCALE = HEAD_DIM ** -0.5
VMEM_LIMIT_BYTES = 56 * 1024 * 1024
NEG_INF = float("-inf")


def _params(n_axes):
    return pltpu.CompilerParams(
        dimension_semantics=("arbitrary",) * n_axes,
        vmem_limit_bytes=VMEM_LIMIT_BYTES)


def _resident(shape):
    nd = len(shape)
    return pl.BlockSpec(shape, lambda *_: (0,) * nd, pipeline_mode=pl.Buffered(1))


def _rms(x, g):
    ms = jnp.mean(x * x, axis=-1, keepdims=True)
    return x * lax.rsqrt(ms + EPS) * g


def _dot(a, b):
    return jnp.dot(a, b, preferred_element_type=F32)


def _dot_nt(a, b):
    return lax.dot_general(a, b, (((1,), (1,)), ((), ())), preferred_element_type=F32)


def _ffn_kernel(x_ref, g_ref, win_ref, wout_ref, o_ref, h_scr, *, d_ff, chunk):
    x = x_ref[...]
    xn = _rms(x, g_ref[...]).astype(BF16)
    for c in range(d_ff // chunk):
        a = _dot(xn, win_ref[:, c * chunk:(c + 1) * chunk])
        b = _dot(xn, win_ref[:, d_ff + c * chunk:d_ff + (c + 1) * chunk])
        h_scr[:, c * chunk:(c + 1) * chunk] = (a * jax.nn.sigmoid(a) * b).astype(BF16)
    o_ref[...] = x + 0.5 * _dot(h_scr[...], wout_ref[...])


def _ffn_half(x, g, w_in, w_out, tm):
    n, d = x.shape
    d_ff = w_out.shape[0]
    chunk = 256
    assert d_ff % chunk == 0 and n % tm == 0
    return pl.pallas_call(
        functools.partial(_ffn_kernel, d_ff=d_ff, chunk=chunk),
        grid=(n // tm,),
        in_specs=[pl.BlockSpec((tm, d), lambda i: (i, 0)),
                  _resident((1, d)), _resident(w_in.shape), _resident(w_out.shape)],
        out_specs=pl.BlockSpec((tm, d), lambda i: (i, 0)),
        out_shape=jax.ShapeDtypeStruct((n, d), F32),
        scratch_shapes=[pltpu.VMEM((tm, d_ff), BF16)],
        compiler_params=_params(1),
        name="ffn_half",
    )(x, g.reshape(1, d), w_in, w_out)


def _ple_kernel(x_ref, p_ref, wup_ref, gpost_ref, ggate_ref, wgate_ref, o_ref):
    x = x_ref[...]
    e = _rms(_dot(p_ref[...].astype(BF16), wup_ref[...]), gpost_ref[...])
    gate = jax.nn.sigmoid(_dot(_rms(x, ggate_ref[...]).astype(BF16), wgate_ref[...]))
    o_ref[...] = x + gate * e


def _ple(x, p, w_up, g_post, g_gate, w_gate, tm):
    n, d = x.shape
    pd = p.shape[1]
    return pl.pallas_call(
        _ple_kernel,
        grid=(n // tm,),
        in_specs=[pl.BlockSpec((tm, d), lambda i: (i, 0)),
                  pl.BlockSpec((tm, pd), lambda i: (i, 0)),
                  _resident(w_up.shape), _resident((1, d)), _resident((1, d)),
                  _resident(w_gate.shape)],
        out_specs=pl.BlockSpec((tm, d), lambda i: (i, 0)),
        out_shape=jax.ShapeDtypeStruct((n, d), F32),
        compiler_params=_params(1),
        name="per_layer_embed",
    )(x, p, w_up, g_post.reshape(1, d), g_gate.reshape(1, d), w_gate)


def _linres_kernel(x_ref, op_ref, os_ref, w_ref, y_ref, *, n_prompt_tiles):
    i = pl.program_id(0)

    @pl.when(i < n_prompt_tiles)
    def _():
        y_ref[...] = x_ref[...] + _dot(op_ref[...].astype(BF16), w_ref[...])

    @pl.when(i >= n_prompt_tiles)
    def _():
        y_ref[...] = x_ref[...] + _dot(os_ref[...].astype(BF16), w_ref[...])


def _linear_residual(x, o_prompt, o_sample, w, tm):
    n, d = x.shape
    npt = o_prompt.shape[0] // tm
    assert o_prompt.shape[0] % tm == 0 and o_sample.shape[0] % tm == 0
    return pl.pallas_call(
        functools.partial(_linres_kernel, n_prompt_tiles=npt),
        grid=(n // tm,),
        in_specs=[pl.BlockSpec((tm, d), lambda i: (i, 0)),
                  pl.BlockSpec((tm, o_prompt.shape[1]), lambda i: (jnp.minimum(i, npt - 1), 0)),
                  pl.BlockSpec((tm, o_sample.shape[1]), lambda i: (jnp.maximum(i - npt, 0), 0)),
                  _resident(w.shape)],
        out_specs=pl.BlockSpec((tm, d), lambda i: (i, 0)),
        out_shape=jax.ShapeDtypeStruct((n, d), F32),
        compiler_params=_params(1),
        name="linear_residual",
    )(x, o_prompt, o_sample, w)


def _headnorm_rope(x, g2, cos, sin_a, sin_b, ones_blk):
    xx = x * x
    hi = xx.astype(BF16)
    lo = (xx - hi.astype(F32)).astype(BF16)
    ss = _dot(hi, ones_blk) + _dot(lo, ones_blk)
    y = x * lax.rsqrt(ss * (1.0 / HEAD_DIM) + EPS) * g2
    half = HEAD_DIM // 8
    return (y * cos + pltpu.roll(y, LANES - half, 1) * sin_a
            + pltpu.roll(y, half, 1) * sin_b)


def _proj_kernel(*refs, segments, n_gains):
    x_ref, g_ref, w_ref = refs[:3]
    gain_refs = refs[3:3 + n_gains]
    cos_ref, sa_ref, sb_ref, ones_ref = refs[3 + n_gains:7 + n_gains]
    out_refs = refs[7 + n_gains:]
    xn = _rms(x_ref[...], g_ref[...]).astype(BF16)
    y = _dot(xn, w_ref[...])
    cos, sin_a, sin_b, ones_blk = cos_ref[...], sa_ref[...], sb_ref[...], ones_ref[...]
    col = 0
    for o_ref, (n_chunks, gain_idx, scale) in zip(out_refs, segments):
        for j in range(n_chunks):
            c = y[:, col:col + LANES]
            if gain_idx is not None:
                c = _headnorm_rope(c, gain_refs[gain_idx][...], cos, sin_a, sin_b, ones_blk)
                if scale != 1.0:
                    c = c * scale
            o_ref[:, j * LANES:(j + 1) * LANES] = c
            col += LANES


def _project(x, g, w, gains, segments, rope, tm, name):
    n, d = x.shape
    cos_t, sa_t, sb_t, tab_map = rope
    ones_blk = (jnp.arange(LANES)[:, None] // HEAD_DIM
                == jnp.arange(LANES)[None, :] // HEAD_DIM).astype(BF16)
    gains2 = [jnp.tile(gv.reshape(1, HEAD_DIM), (1, LANES // HEAD_DIM)) for gv in gains]
    tab_spec = pl.BlockSpec((tm, LANES), tab_map)
    assert sum(s[0] for s in segments) * LANES == w.shape[1]
    return pl.pallas_call(
        functools.partial(_proj_kernel, segments=tuple(segments), n_gains=len(gains)),
        grid=(n // tm,),
        in_specs=[pl.BlockSpec((tm, d), lambda i: (i, 0)), _resident((1, d)),
                  _resident(w.shape)]
                 + [_resident((1, LANES))] * len(gains)
                 + [tab_spec, tab_spec, tab_spec, _resident((LANES, LANES))],
        out_specs=[pl.BlockSpec((tm, s[0] * LANES), lambda i: (i, 0)) for s in segments],
        out_shape=[jax.ShapeDtypeStruct((n, s[0] * LANES), F32) for s in segments],
        compiler_params=_params(1),
        name=name,
    )(x, g.reshape(1, d), w, *gains2, cos_t, sa_t, sb_t, ones_blk)


def _rope_tables(seq, n_sample_pos, past_len, tm):
    half = HEAD_DIM // 8
    inv_freq = ROPE_THETA ** (-jnp.arange(half, dtype=F32) / half)
    pos = jnp.concatenate([
        jnp.arange(seq, dtype=jnp.int32),
        past_len + (jnp.arange(tm, dtype=jnp.int32) % n_sample_pos)])
    ang = pos.astype(F32)[:, None] * inv_freq[None, :]
    cos, sin = jnp.cos(ang), jnp.sin(ang)
    rows = pos.shape[0]
    pad = jnp.zeros((rows, HEAD_DIM - 2 * half), F32)
    zero = jnp.zeros((rows, half), F32)
    reps = (1, LANES // HEAD_DIM)
    cos_t = jnp.tile(jnp.concatenate([cos, cos, pad + 1.0], axis=1), reps)
    sin_a = jnp.tile(jnp.concatenate([-sin, zero, pad], axis=1), reps)
    sin_b = jnp.tile(jnp.concatenate([zero, sin, pad], axis=1), reps)
    return cos_t, sin_a, sin_b


def _flash_update(s, m_scr):
    m_prev = m_scr[...]
    m_next = jnp.maximum(m_prev, jnp.max(s, axis=1, keepdims=True))
    p = jnp.exp(s - jnp.tile(m_next, (1, s.shape[1] // LANES)))
    alpha = jnp.exp(m_prev - m_next)
    m_scr[...] = m_next
    return p.astype(BF16), jnp.tile(alpha, (1, 2))


def _with_ones(v):
    return jnp.concatenate([v, jnp.ones(v.shape, v.dtype)], axis=1)


def _diff_lambda(lq1_ref, lk1_ref, lq2_ref, lk2_ref, lam_init):
    return (jnp.exp(jnp.sum(lq1_ref[...] * lk1_ref[...], axis=1, keepdims=True))
            - jnp.exp(jnp.sum(lq2_ref[...] * lk2_ref[...], axis=1, keepdims=True))
            + lam_init)


def _sub_norm(o1, o2, lam, g_sub, lam_init):
    d = o1 - lam * o2
    return _rms(d, g_sub) * (1.0 - lam_init)


def _aprompt_kernel(q_ref, k_ref, v_ref, lq1_ref, lk1_ref, lq2_ref, lk2_ref, gsub_ref,
                    o_ref, q4_scr, sa_scr, sb_scr, m_scr, acc_scr, *, blk, lam_init):
    qi = pl.program_id(2)
    lane = lax.broadcasted_iota(jnp.int32, (blk, LANES), 1)
    first = lane < HEAD_DIM
    for g in range(2):
        qg = q_ref[:, g * LANES:(g + 1) * LANES]
        q4_scr[g * blk:(g + 1) * blk, :] = jnp.where(first, qg, 0.0).astype(BF16)
        q4_scr[(2 + g) * blk:(3 + g) * blk, :] = jnp.where(first, 0.0, qg).astype(BF16)
    m_scr[...] = jnp.full(m_scr.shape, NEG_INF, F32)
    acc_scr[...] = jnp.zeros(acc_scr.shape, F32)

    def scores(block):
        kb = k_ref[pl.ds(pl.multiple_of(block * blk, blk), blk), :].astype(BF16)
        return _dot_nt(q4_scr[...], kb)

    def consume(s_ref, block):
        vb = v_ref[pl.ds(pl.multiple_of(block * blk, blk), blk), :].astype(BF16)
        p, alpha = _flash_update(s_ref[...], m_scr)
        acc_scr[...] = alpha * acc_scr[...] + _dot(p, _with_ones(vb))

    s = scores(qi)
    tok = jnp.bitwise_and(lax.broadcasted_iota(jnp.int32, s.shape, 0), blk - 1)
    key = lax.broadcasted_iota(jnp.int32, s.shape, 1)
    sa_scr[...] = jnp.where(key <= tok, s, NEG_INF)

    last = jnp.maximum(qi - 1, 0)

    def body(u, carry):
        sb_scr[...] = scores(jnp.minimum(2 * u, last))
        consume(sa_scr, jnp.where(u == 0, qi, 2 * u - 1))

        @pl.when(2 * u + 1 <= qi)
        def _():
            sa_scr[...] = scores(jnp.minimum(2 * u + 1, last))
            consume(sb_scr, 2 * u)

        return carry

    lax.fori_loop(0, qi // 2 + 1, body, 0)

    o = acc_scr[:, :LANES] / acc_scr[:, LANES:]
    lam = _diff_lambda(lq1_ref, lk1_ref, lq2_ref, lk2_ref, lam_init)
    for g in range(2):
        o1 = o[g * blk:(g + 1) * blk, :]
        o2 = o[(2 + g) * blk:(3 + g) * blk, :]
        o_ref[:, g * LANES:(g + 1) * LANES] = _sub_norm(o1, o2, lam, gsub_ref[...], lam_init)


def _a_prompt_attention(q, k, v, lam_vecs, g_sub, lam_init, batch, seq, blk):
    n_kv = k.shape[1] // LANES
    nq = seq // blk
    vec = lambda a: a.reshape(1, -1)
    return pl.pallas_call(
        functools.partial(_aprompt_kernel, blk=blk, lam_init=lam_init),
        grid=(batch, n_kv, nq),
        in_specs=[pl.BlockSpec((blk, 2 * LANES), lambda b, h, i: (b * nq + i, h)),
                  pl.BlockSpec((seq, LANES), lambda b, h, i: (b, h)),
                  pl.BlockSpec((seq, LANES), lambda b, h, i: (b, h))]
                 + [_resident((1, HEAD_DIM))] * 4 + [_resident((1, LANES))],
        out_specs=pl.BlockSpec((blk, 2 * LANES), lambda b, h, i: (b * nq + i, h)),
        out_shape=jax.ShapeDtypeStruct((batch * seq, 2 * n_kv * LANES), F32),
        scratch_shapes=[pltpu.VMEM((4 * blk, LANES), BF16),
                        pltpu.VMEM((4 * blk, blk), F32),
                        pltpu.VMEM((4 * blk, blk), F32),
                        pltpu.VMEM((4 * blk, LANES), F32),
                        pltpu.VMEM((4 * blk, 2 * LANES), F32)],
        compiler_params=_params(3),
        name="a_prompt_attention",
    )(q, k, v, *[vec(a) for a in lam_vecs], vec(g_sub))


def _asample_kernel(pt_ref, q_ref, kn_ref, vn_ref, lq1_ref, lk1_ref, lq2_ref, lk2_ref,
                    gsub_ref, *rest, n_pages_step, n_kv, page, lam_init):
    del pt_ref
    kt_pages = rest[:n_pages_step]
    v_pages = rest[n_pages_step:2 * n_pages_step]
    o_ref, qbd_scr, ktbuf, vbuf, m_scr, acc_scr = rest[2 * n_pages_step:]
    j = pl.program_id(1)
    t = q_ref.shape[0]
    rows_kv = 4 * t

    @pl.when(j == 0)
    def _():
        vbuf[:, :, LANES:] = jnp.ones((n_kv, n_pages_step * page, LANES), BF16)
        qbd_scr[...] = jnp.zeros(qbd_scr.shape, F32)
        lane = lax.broadcasted_iota(jnp.int32, (t, LANES), 1)
        first = lane < HEAD_DIM
        for h in range(n_kv):
            for c in range(2):
                for g in range(2):
                    r0 = ((h * 2 + c) * 2 + g) * t
                    qg = q_ref[:, (2 * h + g) * LANES:(2 * h + g + 1) * LANES]
                    keep = first if c == 0 else jnp.logical_not(first)
                    qbd_scr[r0:r0 + t, h * LANES:(h + 1) * LANES] = jnp.where(keep, qg, 0.0)
        m_scr[...] = jnp.full(m_scr.shape, NEG_INF, F32)
        acc_scr[...] = jnp.zeros(acc_scr.shape, F32)

    def accumulate(s, v_of):
        p, alpha = _flash_update(s, m_scr)
        for h in range(n_kv):
            r = slice(h * rows_kv, (h + 1) * rows_kv)
            acc_scr[r, :] = alpha[r, :] * acc_scr[r, :] + _dot(p[r, :], v_of(h))

    qbd = qbd_scr[...].astype(BF16)
    for r in range(n_pages_step):
        ktbuf[:, r * page:(r + 1) * page] = kt_pages[r][...].astype(BF16)
        for h in range(n_kv):
            vbuf[h, r * page:(r + 1) * page, :LANES] = (
                v_pages[r][pl.ds(h, page, stride=n_kv), :].astype(BF16))
    accumulate(_dot(qbd, ktbuf[...]), lambda h: vbuf[h])

    @pl.when(j == pl.num_programs(1) - 1)
    def _():
        pad = jnp.zeros((LANES - t, kn_ref.shape[1]), F32)
        kn = jnp.concatenate([kn_ref[...], pad], axis=0).astype(BF16)
        vn = jnp.concatenate([vn_ref[...], pad], axis=0).astype(BF16)
        s = _dot_nt(qbd, kn)
        tok = jnp.bitwise_and(lax.broadcasted_iota(jnp.int32, s.shape, 0), t - 1)
        key = lax.broadcasted_iota(jnp.int32, s.shape, 1)
        s = jnp.where(key <= tok, s, NEG_INF)
        accumulate(s, lambda h: _with_ones(vn[:, h * LANES:(h + 1) * LANES]))
        o = acc_scr[:, :LANES] / acc_scr[:, LANES:]
        lam = _diff_lambda(lq1_ref, lk1_ref, lq2_ref, lk2_ref, lam_init)
        for h in range(n_kv):
            for g in range(2):
                r1 = ((h * 2 + 0) * 2 + g) * t
                r2 = ((h * 2 + 1) * 2 + g) * t
                o_ref[:, (2 * h + g) * LANES:(2 * h + g + 1) * LANES] = _sub_norm(
                    o[r1:r1 + t, :], o[r2:r2 + t, :], lam, gsub_ref[...], lam_init)


def _a_sample_attention(q, k, v, cache_kt, cache_v, layer, page_table, lam_vecs, g_sub,
                        lam_init, n_prompt, dec_b, dec_t):
    n_pages = page_table.shape[1]
    kw, page = cache_kt.shape[2], cache_kt.shape[3]
    n_kv = kw // LANES
    assert 4 * dec_t * n_kv == LANES and page == LANES
    assert cache_v.shape[2:] == (page * n_kv, LANES)
    n_pages_step = math.gcd(n_pages, 16)
    row0 = n_prompt // dec_t
    vec = lambda a: a.reshape(1, -1)

    def page_spec(r):
        return pl.BlockSpec(
            (None, None, kw, page),
            lambda b, j, pt: (layer, pt[b, j * n_pages_step + r], 0, 0))

    tok_map = lambda b, j, pt: (row0 + b, 0)
    grid_spec = pltpu.PrefetchScalarGridSpec(
        num_scalar_prefetch=1,
        grid=(dec_b, n_pages // n_pages_step),
        in_specs=[pl.BlockSpec((dec_t, q.shape[1]), tok_map),
                  pl.BlockSpec((dec_t, kw), tok_map),
                  pl.BlockSpec((dec_t, kw), tok_map)]
                 + [pl.BlockSpec((1, HEAD_DIM), lambda b, j, pt: (0, 0))] * 4
                 + [pl.BlockSpec((1, LANES), lambda b, j, pt: (0, 0))]
                 + [page_spec(r) for r in range(n_pages_step)] * 2,
        out_specs=pl.BlockSpec((dec_t, q.shape[1]), lambda b, j, pt: (b, 0)),
        scratch_shapes=[pltpu.VMEM((LANES, kw), F32),
                        pltpu.VMEM((kw, n_pages_step * page), BF16),
                        pltpu.VMEM((n_kv, n_pages_step * page, 2 * LANES), BF16),
                        pltpu.VMEM((LANES, LANES), F32),
                        pltpu.VMEM((LANES, 2 * LANES), F32)])
    return pl.pallas_call(
        functools.partial(_asample_kernel, n_pages_step=n_pages_step, n_kv=n_kv, page=page,
                          lam_init=lam_init),
        grid_spec=grid_spec,
        out_shape=jax.ShapeDtypeStruct((dec_b * dec_t, q.shape[1]), F32),
        compiler_params=_params(2),
        name="a_sample_attention",
    )(page_table, q, k, v, *[vec(a) for a in lam_vecs], vec(g_sub),
      *([cache_kt] * n_pages_step), *([cache_v] * n_pages_step))


def _dup_head(x, h):
    lane = lax.broadcasted_iota(jnp.int32, x.shape, 1)
    keep = (lane < HEAD_DIM) if h == 0 else (lane >= HEAD_DIM)
    return jnp.where(keep, x, pltpu.roll(x, HEAD_DIM, 1))


def _swa_heads(q_ref, k2, v2, sink_ref, o_ref, valid, t, n_group):
    lane = lax.broadcasted_iota(jnp.int32, (t, LANES), 1)
    first = lane < HEAD_DIM
    n_kv = k2.shape[1] // HEAD_DIM
    pairs = n_group // 2
    for h in range(n_kv):
        kd = _dup_head(k2, h).astype(BF16)
        vd = _dup_head(v2, h).astype(BF16)
        parts = []
        for r in range(n_group):
            c = h * pairs + r // 2
            q2 = q_ref[:, c * LANES:(c + 1) * LANES]
            parts.append(jnp.where(first if r % 2 == 0 else jnp.logical_not(first), q2, 0.0))
        qs = jnp.concatenate(parts, axis=0).astype(BF16)
        s = _dot_nt(qs, kd)
        ps = []
        for r in range(n_group):
            sr = jnp.where(valid, s[r * t:(r + 1) * t, :], NEG_INF)
            sink = sink_ref[h * n_group + r]
            m = jnp.maximum(jnp.max(sr, axis=1, keepdims=True), sink)
            p = jnp.exp(sr - m)
            l = jnp.sum(p, axis=1, keepdims=True) + jnp.exp(sink - m)
            ps.append(p / l)
        o = _dot(jnp.concatenate(ps, axis=0).astype(BF16), vd)
        for pr in range(pairs):
            c = h * pairs + pr
            o_ref[:, c * LANES:(c + 1) * LANES] = jnp.where(
                first, o[(2 * pr) * t:(2 * pr + 1) * t, :], o[(2 * pr + 1) * t:(2 * pr + 2) * t, :])


def _bprompt_kernel(q_ref, kp_ref, kc_ref, vp_ref, vc_ref, sink_ref, o_ref, *, n_group):
    i = pl.program_id(1)
    t = q_ref.shape[0]
    k2 = jnp.concatenate([kp_ref[...], kc_ref[...]], axis=0)
    v2 = jnp.concatenate([vp_ref[...], vc_ref[...]], axis=0)
    tok = lax.broadcasted_iota(jnp.int32, (t, 2 * t), 0)
    key = lax.broadcasted_iota(jnp.int32, (t, 2 * t), 1)
    valid = (key > tok) & (key <= tok + t) & (key >= jnp.where(i > 0, 0, t))
    _swa_heads(q_ref, k2, v2, sink_ref, o_ref, valid, t, n_group)


def _b_prompt_attention(q, kb, vb, sinks, batch, seq):
    t = WINDOW
    nb = seq // t
    n_group = (q.shape[1] // HEAD_DIM) // (kb.shape[1] // HEAD_DIM)
    cur = lambda b, i: (b * nb + i, 0)
    prev = lambda b, i: (b * nb + jnp.maximum(i - 1, 0), 0)
    return pl.pallas_call(
        functools.partial(_bprompt_kernel, n_group=n_group),
        grid=(batch, nb),
        in_specs=[pl.BlockSpec((t, q.shape[1]), cur),
                  pl.BlockSpec((t, LANES), prev), pl.BlockSpec((t, LANES), cur),
                  pl.BlockSpec((t, LANES), prev), pl.BlockSpec((t, LANES), cur),
                  pl.BlockSpec(memory_space=pltpu.SMEM)],
        out_specs=pl.BlockSpec((t, q.shape[1]), cur),
        out_shape=jax.ShapeDtypeStruct((batch * seq, q.shape[1]), F32),
        compiler_params=_params(2),
        name="b_prompt_attention",
    )(q, kb, kb, vb, vb, sinks)


def _bsample_kernel(q_ref, kw_ref, kn_ref, vw_ref, vn_ref, sink_ref, o_ref, *, n_group):
    t = q_ref.shape[0]
    w = kw_ref.shape[0]
    pad = jnp.zeros((w - t, LANES), F32)
    k2 = jnp.concatenate([kw_ref[...], kn_ref[...], pad], axis=0)
    v2 = jnp.concatenate([vw_ref[...], vn_ref[...], pad], axis=0)
    tok = lax.broadcasted_iota(jnp.int32, (t, 2 * w), 0)
    key = lax.broadcasted_iota(jnp.int32, (t, 2 * w), 1)
    valid = (key > tok + (w - WINDOW)) & (key <= tok + w)
    _swa_heads(q_ref, k2, v2, sink_ref, o_ref, valid, t, n_group)


def _b_sample_attention(q, kb, vb, cache_k_win, cache_v_win, sinks, n_prompt, dec_b, dec_t):
    w = cache_k_win.shape[1]
    row0 = n_prompt // dec_t
    n_group = (q.shape[1] // HEAD_DIM) // (kb.shape[1] // HEAD_DIM)
    tok = lambda b: (row0 + b, 0)
    win = pl.BlockSpec((None, w, LANES), lambda b: (b, 0, 0))
    return pl.pallas_call(
        functools.partial(_bsample_kernel, n_group=n_group),
        grid=(dec_b,),
        in_specs=[pl.BlockSpec((dec_t, q.shape[1]), tok),
                  win, pl.BlockSpec((dec_t, LANES), tok),
                  win, pl.BlockSpec((dec_t, LANES), tok),
                  pl.BlockSpec(memory_space=pltpu.SMEM)],
        out_specs=pl.BlockSpec((dec_t, q.shape[1]), lambda b: (b, 0)),
        out_shape=jax.ShapeDtypeStruct((dec_b * dec_t, q.shape[1]), F32),
        compiler_params=_params(1),
        name="b_sample_attention",
    )(q, cache_k_win.reshape(dec_b, w, LANES), kb, cache_v_win.reshape(dec_b, w, LANES), vb,
      sinks)


def _token_tile(seq, n_sample):
    for tm in (512, 256, 128, 64, 32, 16, 8):
        if seq % tm == 0 and n_sample % tm == 0:
            return tm
    raise ValueError("unsupported token counts")


def kernel(x_prompt, x_sample, cache_k_a, cache_v_a, cache_k_win, cache_v_win, page_table, p_prompt, p_sample, g_ffn1, w_ffn1_in, w_ffn1_out, g_mix, g_ffn2, w_ffn2_in, w_ffn2_out, w_qkv_a, g_q_a, g_k_a, lambda_q1, lambda_k1, lambda_q2, lambda_k2, g_sub_a, w_o_a, g_kv, w_kv, g_k_b, w_q_b, g_q_b, sinks_b, w_o_b, w_ple_up, g_ple_post, g_ple_gate, w_ple_gate):
    batch, seq, d = x_prompt.shape
    dec_b, dec_t, _ = x_sample.shape
    depth = g_ffn1.shape[0]
    n_a = w_qkv_a.shape[0]
    n_pool, page = cache_k_a.shape[1], cache_k_a.shape[2]
    a_kv = cache_k_a.shape[3]
    past_len = page_table.shape[1] * page
    w_buf = cache_k_win.shape[1]
    n_prompt = batch * seq
    n_sample = dec_b * dec_t
    tm = _token_tile(seq, n_sample)
    assert tm % dec_t == 0 and seq % WINDOW == 0 and w_buf == WINDOW

    bf = lambda w: w.astype(BF16)
    x = jnp.concatenate([x_prompt.reshape(n_prompt, d), x_sample.reshape(n_sample, d)], axis=0)
    p_all = jnp.concatenate([p_prompt.reshape(depth, n_prompt, -1),
                             p_sample.reshape(depth, n_sample, -1)], axis=1)
    cache_kt = jnp.transpose(cache_k_a, (0, 1, 3, 4, 5, 2)).reshape(n_a, n_pool, -1, page)
    cache_v = cache_v_a.reshape(n_a, n_pool, page * a_kv, -1)

    n_pt, spt = n_prompt // tm, seq // tm
    tab_map = lambda i: (jnp.where(i < n_pt, i % spt, spt), 0)
    rope = _rope_tables(seq, dec_t, past_len, tm) + (tab_map,)
    attn_blk = next(b for b in (512, 256, WINDOW) if seq % b == 0)

    ka, va = [], []
    kb = vb = None
    for i in range(depth):
        if i == n_a:
            kb, vb = _project(x, g_kv, bf(w_kv), [g_k_b], [(1, 0, 1.0), (1, None, 1.0)],
                              rope, tm, "shared_kv_proj")
        x = _ffn_half(x, g_ffn1[i], bf(w_ffn1_in[i]), bf(w_ffn1_out[i]), tm)
        if i < n_a:
            lam_init = 0.8 - 0.6 * math.exp(-0.3 * i)
            lam_vecs = (lambda_q1[i], lambda_k1[i], lambda_q2[i], lambda_k2[i])
            q, k, v = _project(
                x, g_mix[i], bf(w_qkv_a[i]), [g_q_a[i], g_k_a[i]],
                [(2 * a_kv * 2 * HEAD_DIM // LANES, 0, QK_SCALE),
                 (a_kv * 2 * HEAD_DIM // LANES, 1, 1.0),
                 (a_kv * 2 * HEAD_DIM // LANES, None, 1.0)],
                rope, tm, "a_qkv_proj")
            ka.append(k)
            va.append(v)
            o_p = _a_prompt_attention(q, k, v, lam_vecs, g_sub_a[i], lam_init, batch, seq,
                                      attn_blk)
            o_s = _a_sample_attention(q, k, v, cache_kt, cache_v, i, page_table, lam_vecs,
                                      g_sub_a[i], lam_init, n_prompt, dec_b, dec_t)
            x = _linear_residual(x, o_p, o_s, bf(w_o_a[i]), tm)
        else:
            j = i - n_a
            (q,) = _project(x, g_mix[i], bf(w_q_b[j]), [g_q_b[j]],
                            [(w_q_b.shape[2] // LANES, 0, QK_SCALE)], rope, tm, "b_q_proj")
            o_p = _b_prompt_attention(q, kb, vb, sinks_b[j], batch, seq)
            o_s = _b_sample_attention(q, kb, vb, cache_k_win, cache_v_win, sinks_b[j],
                                      n_prompt, dec_b, dec_t)
            x = _linear_residual(x, o_p, o_s, bf(w_o_b[j]), tm)
        x = _ffn_half(x, g_ffn2[i], bf(w_ffn2_in[i]), bf(w_ffn2_out[i]), tm)
        x = _ple(x, p_all[i], bf(w_ple_up[i]), g_ple_post[i], g_ple_gate[i],
                 bf(w_ple_gate[i]), tm)

    ka, va = jnp.stack(ka), jnp.stack(va)
    b_kv = kb.shape[1] // HEAD_DIM
    kb_p = kb[:n_prompt].reshape(batch, seq, b_kv, HEAD_DIM)
    vb_p = vb[:n_prompt].reshape(batch, seq, b_kv, HEAD_DIM)
    kb_s = kb[n_prompt:].reshape(dec_b, dec_t, b_kv, HEAD_DIM)
    vb_s = vb[n_prompt:].reshape(dec_b, dec_t, b_kv, HEAD_DIM)
    kw_s = jnp.concatenate([cache_k_win, kb_s], axis=1)
    vw_s = jnp.concatenate([cache_v_win, vb_s], axis=1)
    return (x[:n_prompt].reshape(batch, seq, d),
            x[n_prompt:].reshape(dec_b, dec_t, d),
            ka[:, :n_prompt].reshape(n_a, batch, seq, a_kv, 2, HEAD_DIM),
            va[:, :n_prompt].reshape(n_a, batch, seq, a_kv, 2 * HEAD_DIM),
            ka[:, n_prompt:].reshape(n_a, dec_b, dec_t, a_kv, 2, HEAD_DIM),
            va[:, n_prompt:].reshape(n_a, dec_b, dec_t, a_kv, 2 * HEAD_DIM),
            kb_p[:, -w_buf:], vb_p[:, -w_buf:], kw_s[:, -w_buf:], vw_s[:, -w_buf:])
```

```python
import functools
import math

import jax
import jax.numpy as jnp
from jax import lax
from jax.experimental import pallas as pl
from jax.experimental.pallas import tpu as pltpu

F32 = jnp.float32
BF16 = jnp.bfloat16

HEAD_DIM = 64
WINDOW = 128
ROPE_THETA = 500000.0
EPS = 1e-6
LANES = 128
QK_SCALE = HEAD_DIM ** -0.5
VMEM_LIMIT_BYTES = 56 * 1024 * 1024
NEG_INF = float("-inf")


def _params(n_axes):
    return pltpu.CompilerParams(
        dimension_semantics=("arbitrary",) * n_axes,
        vmem_limit_bytes=VMEM_LIMIT_BYTES)


def _resident(shape):
    nd = len(shape)
    return pl.BlockSpec(shape, lambda *_: (0,) * nd, pipeline_mode=pl.Buffered(1))


def _rms(x, g):
    ms = jnp.mean(x * x, axis=-1, keepdims=True)
    return x * lax.rsqrt(ms + EPS) * g


def _dot(a, b):
    return jnp.dot(a, b, preferred_element_type=F32)


def _dot_nt(a, b):
    return lax.dot_general(a, b, (((1,), (1,)), ((), ())), preferred_element_type=F32)


FFN_CHUNK = 256


def _ffn_body(x, g_ref, win_ref, wout_ref, h_scr):
    d_ff = wout_ref.shape[0]
    xn = _rms(x, g_ref[...]).astype(BF16)
    for c in range(d_ff // FFN_CHUNK):
        lo, hi = c * FFN_CHUNK, (c + 1) * FFN_CHUNK
        a = _dot(xn, win_ref[:, lo:hi])
        b = _dot(xn, win_ref[:, d_ff + lo:d_ff + hi])
        h_scr[:, lo:hi] = (a * jax.nn.sigmoid(a) * b).astype(BF16)
    return x + 0.5 * _dot(h_scr[...], wout_ref[...])


def _ffn_kernel(x_ref, g_ref, win_ref, wout_ref, o_ref, h_scr):
    o_ref[...] = _ffn_body(x_ref[...], g_ref, win_ref, wout_ref, h_scr)


def _ffn_half(x, g, w_in, w_out, tm):
    n, d = x.shape
    d_ff = w_out.shape[0]
    assert d_ff % FFN_CHUNK == 0 and n % tm == 0
    return pl.pallas_call(
        _ffn_kernel,
        grid=(n // tm,),
        in_specs=[pl.BlockSpec((tm, d), lambda i: (i, 0)),
                  _resident((1, d)), _resident(w_in.shape), _resident(w_out.shape)],
        out_specs=pl.BlockSpec((tm, d), lambda i: (i, 0)),
        out_shape=jax.ShapeDtypeStruct((n, d), F32),
        scratch_shapes=[pltpu.VMEM((tm, d_ff), BF16)],
        compiler_params=_params(1),
        name="ffn_half",
    )(x, g.reshape(1, d), w_in, w_out)


def _post_mixer_kernel(x_ref, op_ref, os_ref, wo_ref, g_ref, win_ref, wout_ref, p_ref,
                       wup_ref, gpost_ref, ggate_ref, wgate_ref, y_ref, h_scr, x1_scr, *,
                       n_prompt_tiles):
    i = pl.program_id(0)

    @pl.when(i < n_prompt_tiles)
    def _():
        x1_scr[...] = x_ref[...] + _dot(op_ref[...].astype(BF16), wo_ref[...])

    @pl.when(i >= n_prompt_tiles)
    def _():
        x1_scr[...] = x_ref[...] + _dot(os_ref[...].astype(BF16), wo_ref[...])

    x2 = _ffn_body(x1_scr[...], g_ref, win_ref, wout_ref, h_scr)
    e = _rms(_dot(p_ref[...].astype(BF16), wup_ref[...]), gpost_ref[...])
    gate = jax.nn.sigmoid(_dot(_rms(x2, ggate_ref[...]).astype(BF16), wgate_ref[...]))
    y_ref[...] = x2 + gate * e


def _post_mixer(x, o_prompt, o_sample, w_o, g, w_in, w_out, p, w_up, g_post, g_gate, w_gate,
                tm):
    n, d = x.shape
    d_ff = w_out.shape[0]
    npt = o_prompt.shape[0] // tm
    assert o_prompt.shape[0] % tm == 0 and o_sample.shape[0] % tm == 0
    assert d_ff % FFN_CHUNK == 0 and n % tm == 0
    row = lambda i: (i, 0)
    return pl.pallas_call(
        functools.partial(_post_mixer_kernel, n_prompt_tiles=npt),
        grid=(n // tm,),
        in_specs=[pl.BlockSpec((tm, d), row),
                  pl.BlockSpec((tm, o_prompt.shape[1]), lambda i: (jnp.minimum(i, npt - 1), 0)),
                  pl.BlockSpec((tm, o_sample.shape[1]), lambda i: (jnp.maximum(i - npt, 0), 0)),
                  _resident(w_o.shape), _resident((1, d)), _resident(w_in.shape),
                  _resident(w_out.shape),
                  pl.BlockSpec((tm, p.shape[1]), row),
                  _resident(w_up.shape), _resident((1, d)), _resident((1, d)),
                  _resident(w_gate.shape)],
        out_specs=pl.BlockSpec((tm, d), row),
        out_shape=jax.ShapeDtypeStruct((n, d), F32),
        scratch_shapes=[pltpu.VMEM((tm, d_ff), BF16), pltpu.VMEM((tm, d), F32)],
        compiler_params=_params(1),
        name="post_mixer",
    )(x, o_prompt, o_sample, w_o, g.reshape(1, d), w_in, w_out, p, w_up,
      g_post.reshape(1, d), g_gate.reshape(1, d), w_gate)


def _headnorm_rope(x, g2, cos, sin_a, sin_b, ones_blk):
    xx = x * x
    hi = xx.astype(BF16)
    lo = (xx - hi.astype(F32)).astype(BF16)
    ss = _dot(hi, ones_blk) + _dot(lo, ones_blk)
    y = x * lax.rsqrt(ss * (1.0 / HEAD_DIM) + EPS) * g2
    half = HEAD_DIM // 8
    return (y * cos + pltpu.roll(y, LANES - half, 1) * sin_a
            + pltpu.roll(y, half, 1) * sin_b)


def _proj_kernel(*refs, segments, n_gains, cache_layout):
    x_ref, g_ref, w_ref = refs[:3]
    gain_refs = refs[3:3 + n_gains]
    cos_ref, sa_ref, sb_ref, ones_ref = refs[3 + n_gains:7 + n_gains]
    out_refs = refs[7 + n_gains:7 + n_gains + len(segments)]
    xn = _rms(x_ref[...], g_ref[...]).astype(BF16)
    y = _dot(xn, w_ref[...])
    cos, sin_a, sin_b, ones_blk = cos_ref[...], sa_ref[...], sb_ref[...], ones_ref[...]
    tm = x_ref.shape[0]
    col = 0
    for s, (o_ref, (n_chunks, gain_idx, scale)) in enumerate(zip(out_refs, segments)):
        for j in range(n_chunks):
            c = y[:, col:col + LANES]
            if gain_idx is not None:
                c = _headnorm_rope(c, gain_refs[gain_idx][...], cos, sin_a, sin_b, ones_blk)
                if scale != 1.0:
                    c = c * scale
            o_ref[:, j * LANES:(j + 1) * LANES] = c
            if cache_layout and s == 1:
                refs[-2][j * LANES:(j + 1) * LANES, :] = c.T
            if cache_layout and s == 2:
                refs[-1][pl.ds(j, tm, stride=n_chunks), :] = c
            col += LANES


def _project(x, g, w, gains, segments, rope, tab_map, tile0, n_tiles, tm, name,
             cache_layout=None):
    d = x.shape[1]
    n = n_tiles * tm
    cos_t, sa_t, sb_t = rope
    ones_blk = (jnp.arange(LANES)[:, None] // HEAD_DIM
                == jnp.arange(LANES)[None, :] // HEAD_DIM).astype(BF16)
    gains2 = [jnp.tile(gv.reshape(1, HEAD_DIM), (1, LANES // HEAD_DIM)) for gv in gains]
    tab_spec = pl.BlockSpec((tm, LANES), tab_map)
    assert sum(s[0] for s in segments) * LANES == w.shape[1]
    out_specs = [pl.BlockSpec((tm, s[0] * LANES), lambda i: (i, 0)) for s in segments]
    out_shape = [jax.ShapeDtypeStruct((n, s[0] * LANES), F32) for s in segments]
    if cache_layout is not None:
        batch, seq = cache_layout
        spt = seq // tm
        kw, vc = segments[1][0] * LANES, segments[2][0]
        assert n == batch * seq
        out_specs += [pl.BlockSpec((kw, tm), lambda i: (i // spt, i % spt)),
                      pl.BlockSpec((tm * vc, LANES), lambda i: (i, 0))]
        out_shape += [jax.ShapeDtypeStruct((batch * kw, seq), F32),
                      jax.ShapeDtypeStruct((n * vc, LANES), F32)]
    return pl.pallas_call(
        functools.partial(_proj_kernel, segments=tuple(segments), n_gains=len(gains),
                          cache_layout=cache_layout is not None),
        grid=(n_tiles,),
        in_specs=[pl.BlockSpec((tm, d), lambda i: (tile0 + i, 0)), _resident((1, d)),
                  _resident(w.shape)]
                 + [_resident((1, LANES))] * len(gains)
                 + [tab_spec, tab_spec, tab_spec, _resident((LANES, LANES))],
        out_specs=out_specs,
        out_shape=out_shape,
        compiler_params=_params(1),
        name=name,
    )(x, g.reshape(1, d), w, *gains2, cos_t, sa_t, sb_t, ones_blk)


def _rope_tables(seq, n_sample_pos, past_len, tm):
    half = HEAD_DIM // 8
    inv_freq = ROPE_THETA ** (-jnp.arange(half, dtype=F32) / half)
    pos = jnp.concatenate([
        jnp.arange(seq, dtype=jnp.int32),
        past_len + (jnp.arange(tm, dtype=jnp.int32) % n_sample_pos)])
    ang = pos.astype(F32)[:, None] * inv_freq[None, :]
    cos, sin = jnp.cos(ang), jnp.sin(ang)
    rows = pos.shape[0]
    pad = jnp.zeros((rows, HEAD_DIM - 2 * half), F32)
    zero = jnp.zeros((rows, half), F32)
    reps = (1, LANES // HEAD_DIM)
    cos_t = jnp.tile(jnp.concatenate([cos, cos, pad + 1.0], axis=1), reps)
    sin_a = jnp.tile(jnp.concatenate([-sin, zero, pad], axis=1), reps)
    sin_b = jnp.tile(jnp.concatenate([zero, sin, pad], axis=1), reps)
    return cos_t, sin_a, sin_b


def _flash_update(s, m_scr):
    m_prev = m_scr[...]
    m_next = jnp.maximum(m_prev, jnp.max(s, axis=1, keepdims=True))
    p = jnp.exp(s - jnp.tile(m_next, (1, s.shape[1] // LANES)))
    alpha = jnp.exp(m_prev - m_next)
    m_scr[...] = m_next
    return p.astype(BF16), jnp.tile(alpha, (1, 2))


def _with_ones(v):
    return jnp.concatenate([v, jnp.ones(v.shape, v.dtype)], axis=1)


def _diff_lambda(lq1_ref, lk1_ref, lq2_ref, lk2_ref, lam_init):
    return (jnp.exp(jnp.sum(lq1_ref[...] * lk1_ref[...], axis=1, keepdims=True))
            - jnp.exp(jnp.sum(lq2_ref[...] * lk2_ref[...], axis=1, keepdims=True))
            + lam_init)


def _sub_norm(o1, o2, lam, g_sub, lam_init):
    d = o1 - lam * o2
    return _rms(d, g_sub) * (1.0 - lam_init)


def _aprompt_kernel(q_ref, k_ref, v_ref, lq1_ref, lk1_ref, lq2_ref, lk2_ref, gsub_ref,
                    o_ref, q4_scr, sa_scr, sb_scr, m_scr, acc_scr, *, blk, lam_init):
    qi = pl.program_id(2)
    lane = lax.broadcasted_iota(jnp.int32, (blk, LANES), 1)
    first = lane < HEAD_DIM
    for g in range(2):
        qg = q_ref[:, g * LANES:(g + 1) * LANES]
        q4_scr[g * blk:(g + 1) * blk, :] = jnp.where(first, qg, 0.0).astype(BF16)
        q4_scr[(2 + g) * blk:(3 + g) * blk, :] = jnp.where(first, 0.0, qg).astype(BF16)
    m_scr[...] = jnp.full(m_scr.shape, NEG_INF, F32)
    acc_scr[...] = jnp.zeros(acc_scr.shape, F32)

    def scores(block):
        kb = k_ref[pl.ds(pl.multiple_of(block * blk, blk), blk), :].astype(BF16)
        return _dot_nt(q4_scr[...], kb)

    def consume(s_ref, block):
        vb = v_ref[pl.ds(pl.multiple_of(block * blk, blk), blk), :].astype(BF16)
        p, alpha = _flash_update(s_ref[...], m_scr)
        acc_scr[...] = alpha * acc_scr[...] + _dot(p, _with_ones(vb))

    s = scores(qi)
    tok = jnp.bitwise_and(lax.broadcasted_iota(jnp.int32, s.shape, 0), blk - 1)
    key = lax.broadcasted_iota(jnp.int32, s.shape, 1)
    sa_scr[...] = jnp.where(key <= tok, s, NEG_INF)

    last = jnp.maximum(qi - 1, 0)

    def body(u, carry):
        sb_scr[...] = scores(jnp.minimum(2 * u, last))
        consume(sa_scr, jnp.where(u == 0, qi, 2 * u - 1))

        @pl.when(2 * u + 1 <= qi)
        def _():
            sa_scr[...] = scores(jnp.minimum(2 * u + 1, last))
            consume(sb_scr, 2 * u)

        return carry

    lax.fori_loop(0, qi // 2 + 1, body, 0)

    o = acc_scr[:, :LANES] / acc_scr[:, LANES:]
    lam = _diff_lambda(lq1_ref, lk1_ref, lq2_ref, lk2_ref, lam_init)
    for g in range(2):
        o1 = o[g * blk:(g + 1) * blk, :]
        o2 = o[(2 + g) * blk:(3 + g) * blk, :]
        o_ref[:, g * LANES:(g + 1) * LANES] = _sub_norm(o1, o2, lam, gsub_ref[...], lam_init)


def _a_prompt_attention(q, k, v, lam_vecs, g_sub, lam_init, batch, seq, blk):
    n_kv = k.shape[1] // LANES
    nq = seq // blk
    vec = lambda a: a.reshape(1, -1)
    return pl.pallas_call(
        functools.partial(_aprompt_kernel, blk=blk, lam_init=lam_init),
        grid=(batch, n_kv, nq),
        in_specs=[pl.BlockSpec((blk, 2 * LANES), lambda b, h, i: (b * nq + i, h)),
                  pl.BlockSpec((seq, LANES), lambda b, h, i: (b, h)),
                  pl.BlockSpec((seq, LANES), lambda b, h, i: (b, h))]
                 + [_resident((1, HEAD_DIM))] * 4 + [_resident((1, LANES))],
        out_specs=pl.BlockSpec((blk, 2 * LANES), lambda b, h, i: (b * nq + i, h)),
        out_shape=jax.ShapeDtypeStruct((batch * seq, 2 * n_kv * LANES), F32),
        scratch_shapes=[pltpu.VMEM((4 * blk, LANES), BF16),
                        pltpu.VMEM((4 * blk, blk), F32),
                        pltpu.VMEM((4 * blk, blk), F32),
                        pltpu.VMEM((4 * blk, LANES), F32),
                        pltpu.VMEM((4 * blk, 2 * LANES), F32)],
        compiler_params=_params(3),
        name="a_prompt_attention",
    )(q, k, v, *[vec(a) for a in lam_vecs], vec(g_sub))


def _asample_kernel(pt_ref, q_ref, kn_ref, vn_ref, lq1_ref, lk1_ref, lq2_ref, lk2_ref,
                    gsub_ref, *rest, n_seq, n_pages_step, n_kv, page, lam_init):
    del pt_ref
    n_win = n_seq * n_pages_step
    kt_pages = rest[:n_win]
    v_pages = rest[n_win:2 * n_win]
    o_ref, qbd_scr, ktbuf, vbuf, m_scr, acc_scr = rest[2 * n_win:]
    j = pl.program_id(1)
    t = q_ref.shape[0] // n_seq
    rows_kv = 4 * t

    @pl.when(j == 0)
    def _():
        vbuf[:, :, :, LANES:] = jnp.ones((n_seq, n_kv, n_pages_step * page, LANES), BF16)
        qbd_scr[...] = jnp.zeros(qbd_scr.shape, F32)
        lane = lax.broadcasted_iota(jnp.int32, (t, LANES), 1)
        first = lane < HEAD_DIM
        for b in range(n_seq):
            for h in range(n_kv):
                for c in range(2):
                    for g in range(2):
                        r0 = ((h * 2 + c) * 2 + g) * t
                        qg = q_ref[b * t:(b + 1) * t, (2 * h + g) * LANES:(2 * h + g + 1) * LANES]
                        keep = first if c == 0 else jnp.logical_not(first)
                        qbd_scr[b, r0:r0 + t, h * LANES:(h + 1) * LANES] = jnp.where(keep, qg, 0.0)
        m_scr[...] = jnp.full(m_scr.shape, NEG_INF, F32)
        acc_scr[...] = jnp.zeros(acc_scr.shape, F32)

    def accumulate(b, s, v_of):
        p, alpha = _flash_update(s, m_scr.at[b])
        for h in range(n_kv):
            r = slice(h * rows_kv, (h + 1) * rows_kv)
            acc_scr[b, r, :] = alpha[r, :] * acc_scr[b, r, :] + _dot(p[r, :], v_of(h))

    qbd = [qbd_scr[b].astype(BF16) for b in range(n_seq)]
    for b in range(n_seq):
        for r in range(n_pages_step):
            w = b * n_pages_step + r
            ktbuf[b, :, r * page:(r + 1) * page] = kt_pages[w][...].astype(BF16)
            for h in range(n_kv):
                vbuf[b, h, r * page:(r + 1) * page, :LANES] = (
                    v_pages[w][pl.ds(h, page, stride=n_kv), :].astype(BF16))
        accumulate(b, _dot(qbd[b], ktbuf[b]), lambda h, b=b: vbuf[b, h])

    @pl.when(j == pl.num_programs(1) - 1)
    def _():
        lam = _diff_lambda(lq1_ref, lk1_ref, lq2_ref, lk2_ref, lam_init)
        pad = jnp.zeros((LANES - t, kn_ref.shape[1]), F32)
        for b in range(n_seq):
            kn = jnp.concatenate([kn_ref[b * t:(b + 1) * t, :], pad], axis=0).astype(BF16)
            vn = jnp.concatenate([vn_ref[b * t:(b + 1) * t, :], pad], axis=0).astype(BF16)
            s = _dot_nt(qbd[b], kn)
            tok = jnp.bitwise_and(lax.broadcasted_iota(jnp.int32, s.shape, 0), t - 1)
            key = lax.broadcasted_iota(jnp.int32, s.shape, 1)
            s = jnp.where(key <= tok, s, NEG_INF)
            accumulate(b, s, lambda h, vn=vn: _with_ones(vn[:, h * LANES:(h + 1) * LANES]))
            o = acc_scr[b, :, :LANES] / acc_scr[b, :, LANES:]
            for h in range(n_kv):
                for g in range(2):
                    r1 = ((h * 2 + 0) * 2 + g) * t
                    r2 = ((h * 2 + 1) * 2 + g) * t
                    o_ref[b * t:(b + 1) * t, (2 * h + g) * LANES:(2 * h + g + 1) * LANES] = (
                        _sub_norm(o[r1:r1 + t, :], o[r2:r2 + t, :], lam, gsub_ref[...], lam_init))


def _a_sample_attention(q, k, v, cache_kt, cache_v, layer, page_table, lam_vecs, g_sub,
                        lam_init, dec_b, dec_t):
    n_pages = page_table.shape[1]
    kw, page = cache_kt.shape[2], cache_kt.shape[3]
    n_kv = kw // LANES
    assert 4 * dec_t * n_kv == LANES and page == LANES
    assert cache_v.shape[2:] == (page * n_kv, LANES)
    n_seq = 2 if dec_b % 2 == 0 else 1
    n_pages_step = math.gcd(n_pages, 8)
    rows = n_seq * dec_t
    vec = lambda a: a.reshape(1, -1)

    def page_spec(b, r):
        return pl.BlockSpec(
            (None, None, kw, page),
            lambda i, j, pt: (layer, pt[i * n_seq + b, j * n_pages_step + r], 0, 0))

    pages = [page_spec(b, r) for b in range(n_seq) for r in range(n_pages_step)]
    tok_map = lambda i, j, pt: (i, 0)
    grid_spec = pltpu.PrefetchScalarGridSpec(
        num_scalar_prefetch=1,
        grid=(dec_b // n_seq, n_pages // n_pages_step),
        in_specs=[pl.BlockSpec((rows, q.shape[1]), tok_map),
                  pl.BlockSpec((rows, kw), tok_map),
                  pl.BlockSpec((rows, kw), tok_map)]
                 + [pl.BlockSpec((1, HEAD_DIM), lambda i, j, pt: (0, 0))] * 4
                 + [pl.BlockSpec((1, LANES), lambda i, j, pt: (0, 0))]
                 + pages * 2,
        out_specs=pl.BlockSpec((rows, q.shape[1]), tok_map),
        scratch_shapes=[pltpu.VMEM((n_seq, LANES, kw), F32),
                        pltpu.VMEM((n_seq, kw, n_pages_step * page), BF16),
                        pltpu.VMEM((n_seq, n_kv, n_pages_step * page, 2 * LANES), BF16),
                        pltpu.VMEM((n_seq, LANES, LANES), F32),
                        pltpu.VMEM((n_seq, LANES, 2 * LANES), F32)])
    n_win = n_seq * n_pages_step
    return pl.pallas_call(
        functools.partial(_asample_kernel, n_seq=n_seq, n_pages_step=n_pages_step, n_kv=n_kv,
                          page=page, lam_init=lam_init),
        grid_spec=grid_spec,
        out_shape=jax.ShapeDtypeStruct((dec_b * dec_t, q.shape[1]), F32),
        compiler_params=_params(2),
        name="a_sample_attention",
    )(page_table, q, k, v, *[vec(a) for a in lam_vecs], vec(g_sub),
      *([cache_kt] * n_win), *([cache_v] * n_win))


def _dup_head(x, h):
    lane = lax.broadcasted_iota(jnp.int32, x.shape, 1)
    keep = (lane < HEAD_DIM) if h == 0 else (lane >= HEAD_DIM)
    return jnp.where(keep, x, pltpu.roll(x, HEAD_DIM, 1))


def _swa_heads(q_ref, k2, v2, sink_ref, o_ref, valid, t, n_group):
    lane = lax.broadcasted_iota(jnp.int32, (t, LANES), 1)
    first = lane < HEAD_DIM
    n_kv = k2.shape[1] // HEAD_DIM
    pairs = n_group // 2
    for h in range(n_kv):
        kd = _dup_head(k2, h).astype(BF16)
        vd = _dup_head(v2, h).astype(BF16)
        parts = []
        for r in range(n_group):
            c = h * pairs + r // 2
            q2 = q_ref[:, c * LANES:(c + 1) * LANES]
            parts.append(jnp.where(first if r % 2 == 0 else jnp.logical_not(first), q2, 0.0))
        qs = jnp.concatenate(parts, axis=0).astype(BF16)
        s = _dot_nt(qs, kd)
        ps = []
        for r in range(n_group):
            sr = jnp.where(valid, s[r * t:(r + 1) * t, :], NEG_INF)
            sink = sink_ref[h * n_group + r]
            m = jnp.maximum(jnp.max(sr, axis=1, keepdims=True), sink)
            p = jnp.exp(sr - m)
            l = jnp.sum(p, axis=1, keepdims=True) + jnp.exp(sink - m)
            ps.append(p / l)
        o = _dot(jnp.concatenate(ps, axis=0).astype(BF16), vd)
        for pr in range(pairs):
            c = h * pairs + pr
            o_ref[:, c * LANES:(c + 1) * LANES] = jnp.where(
                first, o[(2 * pr) * t:(2 * pr + 1) * t, :], o[(2 * pr + 1) * t:(2 * pr + 2) * t, :])


def _bprompt_kernel(q_ref, kp_ref, kc_ref, vp_ref, vc_ref, sink_ref, o_ref, *, n_group):
    i = pl.program_id(1)
    t = q_ref.shape[0]
    k2 = jnp.concatenate([kp_ref[...], kc_ref[...]], axis=0)
    v2 = jnp.concatenate([vp_ref[...], vc_ref[...]], axis=0)
    tok = lax.broadcasted_iota(jnp.int32, (t, 2 * t), 0)
    key = lax.broadcasted_iota(jnp.int32, (t, 2 * t), 1)
    valid = (key > tok) & (key <= tok + t) & (key >= jnp.where(i > 0, 0, t))
    _swa_heads(q_ref, k2, v2, sink_ref, o_ref, valid, t, n_group)


def _b_prompt_attention(q, kb, vb, sinks, batch, seq):
    t = WINDOW
    nb = seq // t
    n_group = (q.shape[1] // HEAD_DIM) // (kb.shape[1] // HEAD_DIM)
    cur = lambda b, i: (b * nb + i, 0)
    prev = lambda b, i: (b * nb + jnp.maximum(i - 1, 0), 0)
    return pl.pallas_call(
        functools.partial(_bprompt_kernel, n_group=n_group),
        grid=(batch, nb),
        in_specs=[pl.BlockSpec((t, q.shape[1]), cur),
                  pl.BlockSpec((t, LANES), prev), pl.BlockSpec((t, LANES), cur),
                  pl.BlockSpec((t, LANES), prev), pl.BlockSpec((t, LANES), cur),
                  pl.BlockSpec(memory_space=pltpu.SMEM)],
        out_specs=pl.BlockSpec((t, q.shape[1]), cur),
        out_shape=jax.ShapeDtypeStruct((batch * seq, q.shape[1]), F32),
        compiler_params=_params(2),
        name="b_prompt_attention",
    )(q, kb, kb, vb, vb, sinks)


def _bsample_kernel(q_ref, kw_ref, kn_ref, vw_ref, vn_ref, sink_ref, o_ref, *, n_group):
    t = q_ref.shape[0]
    w = kw_ref.shape[0]
    pad = jnp.zeros((w - t, LANES), F32)
    k2 = jnp.concatenate([kw_ref[...], kn_ref[...], pad], axis=0)
    v2 = jnp.concatenate([vw_ref[...], vn_ref[...], pad], axis=0)
    tok = lax.broadcasted_iota(jnp.int32, (t, 2 * w), 0)
    key = lax.broadcasted_iota(jnp.int32, (t, 2 * w), 1)
    valid = (key > tok + (w - WINDOW)) & (key <= tok + w)
    _swa_heads(q_ref, k2, v2, sink_ref, o_ref, valid, t, n_group)


def _b_sample_attention(q, kb, vb, cache_k_win, cache_v_win, sinks, n_prompt, dec_b, dec_t):
    w = cache_k_win.shape[1]
    row0 = n_prompt // dec_t
    n_group = (q.shape[1] // HEAD_DIM) // (kb.shape[1] // HEAD_DIM)
    tok = lambda b: (row0 + b, 0)
    win = pl.BlockSpec((None, w, LANES), lambda b: (b, 0, 0))
    return pl.pallas_call(
        functools.partial(_bsample_kernel, n_group=n_group),
        grid=(dec_b,),
        in_specs=[pl.BlockSpec((dec_t, q.shape[1]), tok),
                  win, pl.BlockSpec((dec_t, LANES), tok),
                  win, pl.BlockSpec((dec_t, LANES), tok),
                  pl.BlockSpec(memory_space=pltpu.SMEM)],
        out_specs=pl.BlockSpec((dec_t, q.shape[1]), lambda b: (b, 0)),
        out_shape=jax.ShapeDtypeStruct((dec_b * dec_t, q.shape[1]), F32),
        compiler_params=_params(1),
        name="b_sample_attention",
    )(q, cache_k_win.reshape(dec_b, w, LANES), kb, cache_v_win.reshape(dec_b, w, LANES), vb,
      sinks)


def _token_tile(seq, n_sample):
    for tm in (512, 256, 128, 64, 32, 16, 8):
        if seq % tm == 0 and n_sample % tm == 0:
            return tm
    raise ValueError("unsupported token counts")


def kernel(x_prompt, x_sample, cache_k_a, cache_v_a, cache_k_win, cache_v_win, page_table, p_prompt, p_sample, g_ffn1, w_ffn1_in, w_ffn1_out, g_mix, g_ffn2, w_ffn2_in, w_ffn2_out, w_qkv_a, g_q_a, g_k_a, lambda_q1, lambda_k1, lambda_q2, lambda_k2, g_sub_a, w_o_a, g_kv, w_kv, g_k_b, w_q_b, g_q_b, sinks_b, w_o_b, w_ple_up, g_ple_post, g_ple_gate, w_ple_gate):
    batch, seq, d = x_prompt.shape
    dec_b, dec_t, _ = x_sample.shape
    depth = g_ffn1.shape[0]
    n_a = w_qkv_a.shape[0]
    n_pool, page = cache_k_a.shape[1], cache_k_a.shape[2]
    a_kv = cache_k_a.shape[3]
    past_len = page_table.shape[1] * page
    w_buf = cache_k_win.shape[1]
    n_prompt = batch * seq
    n_sample = dec_b * dec_t
    tm = _token_tile(seq, n_sample)
    assert tm % dec_t == 0 and seq % WINDOW == 0 and w_buf == WINDOW

    bf = lambda w: w.astype(BF16)
    x = jnp.concatenate([x_prompt.reshape(n_prompt, d), x_sample.reshape(n_sample, d)], axis=0)
    p_all = jnp.concatenate([p_prompt.reshape(depth, n_prompt, -1),
                             p_sample.reshape(depth, n_sample, -1)], axis=1)
    cache_kt = jnp.transpose(cache_k_a, (0, 1, 3, 4, 5, 2)).reshape(n_a, n_pool, -1, page)
    cache_v = cache_v_a.reshape(n_a, n_pool, page * a_kv, -1)

    n_pt, spt, n_st = n_prompt // tm, seq // tm, n_sample // tm
    rope = _rope_tables(seq, dec_t, past_len, tm)
    prompt_tab = lambda i: (i % spt, 0)
    sample_tab = lambda i: (spt, 0)
    all_tab = lambda i: (jnp.where(i < n_pt, i % spt, spt), 0)
    attn_blk = next(b for b in (512, 256, WINDOW) if seq % b == 0)
    kv_chunks = a_kv * 2 * HEAD_DIM // LANES
    a_segments = [(2 * kv_chunks, 0, QK_SCALE), (kv_chunks, 1, 1.0), (kv_chunks, None, 1.0)]

    ka_p, va_p, ka_s, va_s = [], [], [], []
    kb = vb = None
    for i in range(depth):
        if i == n_a:
            kb, vb = _project(x, g_kv, bf(w_kv), [g_k_b], [(1, 0, 1.0), (1, None, 1.0)],
                              rope, all_tab, 0, n_pt + n_st, tm, "shared_kv_proj")
        x = _ffn_half(x, g_ffn1[i], bf(w_ffn1_in[i]), bf(w_ffn1_out[i]), tm)
        if i < n_a:
            lam_init = 0.8 - 0.6 * math.exp(-0.3 * i)
            lam_vecs = (lambda_q1[i], lambda_k1[i], lambda_q2[i], lambda_k2[i])
            w_qkv, gains = bf(w_qkv_a[i]), [g_q_a[i], g_k_a[i]]
            q, k, v, k_cache, v_cache = _project(
                x, g_mix[i], w_qkv, gains, a_segments, rope, prompt_tab, 0, n_pt, tm,
                "a_qkv_proj_prompt", cache_layout=(batch, seq))
            q_s, k_s, v_s = _project(
                x, g_mix[i], w_qkv, gains, a_segments, rope, sample_tab, n_pt, n_st, tm,
                "a_qkv_proj_sample")
            ka_p.append(k_cache)
            va_p.append(v_cache)
            ka_s.append(k_s)
            va_s.append(v_s)
            o_p = _a_prompt_attention(q, k, v, lam_vecs, g_sub_a[i], lam_init, batch, seq,
                                      attn_blk)
            o_s = _a_sample_attention(q_s, k_s, v_s, cache_kt, cache_v, i, page_table,
                                      lam_vecs, g_sub_a[i], lam_init, dec_b, dec_t)
            w_o = bf(w_o_a[i])
        else:
            j = i - n_a
            (q,) = _project(x, g_mix[i], bf(w_q_b[j]), [g_q_b[j]],
                            [(w_q_b.shape[2] // LANES, 0, QK_SCALE)], rope, all_tab, 0,
                            n_pt + n_st, tm, "b_q_proj")
            o_p = _b_prompt_attention(q, kb, vb, sinks_b[j], batch, seq)
            o_s = _b_sample_attention(q, kb, vb, cache_k_win, cache_v_win, sinks_b[j],
                                      n_prompt, dec_b, dec_t)
            w_o = bf(w_o_b[j])
        x = _post_mixer(x, o_p, o_s, w_o, g_ffn2[i], bf(w_ffn2_in[i]), bf(w_ffn2_out[i]),
                        p_all[i], bf(w_ple_up[i]), g_ple_post[i], g_ple_gate[i],
                        bf(w_ple_gate[i]), tm)

    ka_p = jnp.transpose(jnp.stack(ka_p).reshape(n_a, batch, a_kv, 2, HEAD_DIM, seq),
                         (0, 1, 5, 2, 3, 4))
    va_p = jnp.stack(va_p).reshape(n_a, batch, seq, a_kv, 2 * HEAD_DIM)
    ka_s = jnp.stack(ka_s).reshape(n_a, dec_b, dec_t, a_kv, 2, HEAD_DIM)
    va_s = jnp.stack(va_s).reshape(n_a, dec_b, dec_t, a_kv, 2 * HEAD_DIM)
    b_kv = kb.shape[1] // HEAD_DIM
    kb_p = kb[:n_prompt].reshape(batch, seq, b_kv, HEAD_DIM)
    vb_p = vb[:n_prompt].reshape(batch, seq, b_kv, HEAD_DIM)
    kb_s = kb[n_prompt:].reshape(dec_b, dec_t, b_kv, HEAD_DIM)
    vb_s = vb[n_prompt:].reshape(dec_b, dec_t, b_kv, HEAD_DIM)
    kw_s = jnp.concatenate([cache_k_win, kb_s], axis=1)
    vw_s = jnp.concatenate([cache_v_win, vb_s], axis=1)
    return (x[:n_prompt].reshape(batch, seq, d),
            x[n_prompt:].reshape(dec_b, dec_t, d),
            ka_p, va_p, ka_s, va_s,
            kb_p[:, -w_buf:], vb_p[:, -w_buf:], kw_s[:, -w_buf:], vw_s[:, -w_buf:])
```

```python
import functools
import math

import jax
import jax.numpy as jnp
from jax import lax
from jax.experimental import pallas as pl
from jax.experimental.pallas import tpu as pltpu

F32 = jnp.float32
BF16 = jnp.bfloat16

HEAD_DIM = 64
WINDOW = 128
ROPE_THETA = 500000.0
EPS = 1e-6
LANES = 128
QK_SCALE = HEAD_DIM ** -0.5
VMEM_LIMIT_BYTES = 56 * 1024 * 1024
NEG_INF = float("-inf")


def _params(n_axes):
    return pltpu.CompilerParams(
        dimension_semantics=("arbitrary",) * n_axes,
        vmem_limit_bytes=VMEM_LIMIT_BYTES)


def _resident(shape):
    nd = len(shape)
    return pl.BlockSpec(shape, lambda *_: (0,) * nd, pipeline_mode=pl.Buffered(1))


def _rms(x, g):
    ms = jnp.mean(x * x, axis=-1, keepdims=True)
    return x * lax.rsqrt(ms + EPS) * g


def _dot(a, b):
    return jnp.dot(a, b, preferred_element_type=F32)


def _dot_nt(a, b):
    return lax.dot_general(a, b, (((1,), (1,)), ((), ())), preferred_element_type=F32)


FFN_CHUNK = 256


def _ffn_body(x, g_ref, win_ref, wout_ref, h_scr):
    d_ff = wout_ref.shape[0]
    xn = _rms(x, g_ref[...]).astype(BF16)
    for c in range(d_ff // FFN_CHUNK):
        lo, hi = c * FFN_CHUNK, (c + 1) * FFN_CHUNK
        a = _dot(xn, win_ref[:, lo:hi])
        b = _dot(xn, win_ref[:, d_ff + lo:d_ff + hi])
        h_scr[:, lo:hi] = (a * jax.nn.sigmoid(a) * b).astype(BF16)
    return x + 0.5 * _dot(h_scr[...], wout_ref[...])


def _ffn_kernel(x_ref, g_ref, win_ref, wout_ref, o_ref, h_scr):
    o_ref[...] = _ffn_body(x_ref[...], g_ref, win_ref, wout_ref, h_scr)


def _ffn_join_kernel(xp_ref, xs_ref, g_ref, win_ref, wout_ref, o_ref, h_scr, x_scr, *,
                     n_prompt_tiles):
    i = pl.program_id(0)

    @pl.when(i < n_prompt_tiles)
    def _():
        x_scr[...] = xp_ref[...]

    @pl.when(i >= n_prompt_tiles)
    def _():
        x_scr[...] = xs_ref[...]

    o_ref[...] = _ffn_body(x_scr[...], g_ref, win_ref, wout_ref, h_scr)


def _ffn_half_join(x_prompt, x_sample, g, w_in, w_out, tm):
    d = x_prompt.shape[1]
    d_ff = w_out.shape[0]
    npt, nst = x_prompt.shape[0] // tm, x_sample.shape[0] // tm
    assert x_prompt.shape[0] % tm == 0 and x_sample.shape[0] % tm == 0
    assert d_ff % FFN_CHUNK == 0
    return pl.pallas_call(
        functools.partial(_ffn_join_kernel, n_prompt_tiles=npt),
        grid=(npt + nst,),
        in_specs=[pl.BlockSpec((tm, d), lambda i: (jnp.minimum(i, npt - 1), 0)),
                  pl.BlockSpec((tm, d), lambda i: (jnp.maximum(i - npt, 0), 0)),
                  _resident((1, d)), _resident(w_in.shape), _resident(w_out.shape)],
        out_specs=pl.BlockSpec((tm, d), lambda i: (i, 0)),
        out_shape=jax.ShapeDtypeStruct(((npt + nst) * tm, d), F32),
        scratch_shapes=[pltpu.VMEM((tm, d_ff), BF16), pltpu.VMEM((tm, d), F32)],
        compiler_params=_params(1),
        name="ffn_half_join",
    )(x_prompt, x_sample, g.reshape(1, d), w_in, w_out)


def _ffn_half(x, g, w_in, w_out, tm):
    n, d = x.shape
    d_ff = w_out.shape[0]
    assert d_ff % FFN_CHUNK == 0 and n % tm == 0
    return pl.pallas_call(
        _ffn_kernel,
        grid=(n // tm,),
        in_specs=[pl.BlockSpec((tm, d), lambda i: (i, 0)),
                  _resident((1, d)), _resident(w_in.shape), _resident(w_out.shape)],
        out_specs=pl.BlockSpec((tm, d), lambda i: (i, 0)),
        out_shape=jax.ShapeDtypeStruct((n, d), F32),
        scratch_shapes=[pltpu.VMEM((tm, d_ff), BF16)],
        compiler_params=_params(1),
        name="ffn_half",
    )(x, g.reshape(1, d), w_in, w_out)


def _post_mixer_kernel(x_ref, op_ref, os_ref, wo_ref, g_ref, win_ref, wout_ref, pp_ref, ps_ref,
                       wup_ref, gpost_ref, ggate_ref, wgate_ref, *rest, n_prompt_tiles, split):
    y_refs, (h_scr, x1_scr, p_scr) = rest[:-3], rest[-3:]
    i = pl.program_id(0)

    @pl.when(i < n_prompt_tiles)
    def _():
        x1_scr[...] = x_ref[...] + _dot(op_ref[...].astype(BF16), wo_ref[...])
        p_scr[...] = pp_ref[...].astype(BF16)

    @pl.when(i >= n_prompt_tiles)
    def _():
        x1_scr[...] = x_ref[...] + _dot(os_ref[...].astype(BF16), wo_ref[...])
        p_scr[...] = ps_ref[...].astype(BF16)

    x2 = _ffn_body(x1_scr[...], g_ref, win_ref, wout_ref, h_scr)
    e = _rms(_dot(p_scr[...], wup_ref[...]), gpost_ref[...])
    gate = jax.nn.sigmoid(_dot(_rms(x2, ggate_ref[...]).astype(BF16), wgate_ref[...]))
    y = x2 + gate * e
    if not split:
        y_refs[0][...] = y
    else:
        @pl.when(i < n_prompt_tiles)
        def _():
            y_refs[0][...] = y

        @pl.when(i >= n_prompt_tiles)
        def _():
            y_refs[1][...] = y


def _post_mixer(x, o_prompt, o_sample, w_o, g, w_in, w_out, p_prompt, p_sample, layer, w_up,
                g_post, g_gate, w_gate, tm, split):
    n, d = x.shape
    d_ff = w_out.shape[0]
    pd = p_prompt.shape[2]
    npt, nst = o_prompt.shape[0] // tm, o_sample.shape[0] // tm
    assert o_prompt.shape[0] % tm == 0 and o_sample.shape[0] % tm == 0
    assert d_ff % FFN_CHUNK == 0 and n == (npt + nst) * tm
    row = lambda i: (i, 0)
    prompt_row = lambda i: (jnp.minimum(i, npt - 1), 0)
    sample_row = lambda i: (jnp.maximum(i - npt, 0), 0)
    if split:
        out_specs = [pl.BlockSpec((tm, d), prompt_row), pl.BlockSpec((tm, d), sample_row)]
        out_shape = [jax.ShapeDtypeStruct((npt * tm, d), F32),
                     jax.ShapeDtypeStruct((nst * tm, d), F32)]
    else:
        out_specs = [pl.BlockSpec((tm, d), row)]
        out_shape = [jax.ShapeDtypeStruct((n, d), F32)]
    return pl.pallas_call(
        functools.partial(_post_mixer_kernel, n_prompt_tiles=npt, split=split),
        grid=(n // tm,),
        in_specs=[pl.BlockSpec((tm, d), row),
                  pl.BlockSpec((tm, o_prompt.shape[1]), prompt_row),
                  pl.BlockSpec((tm, o_sample.shape[1]), sample_row),
                  _resident(w_o.shape), _resident((1, d)), _resident(w_in.shape),
                  _resident(w_out.shape),
                  pl.BlockSpec((None, tm, pd), lambda i: (layer, jnp.minimum(i, npt - 1), 0)),
                  pl.BlockSpec((None, tm, pd), lambda i: (layer, jnp.maximum(i - npt, 0), 0)),
                  _resident(w_up.shape), _resident((1, d)), _resident((1, d)),
                  _resident(w_gate.shape)],
        out_specs=out_specs,
        out_shape=out_shape,
        scratch_shapes=[pltpu.VMEM((tm, d_ff), BF16), pltpu.VMEM((tm, d), F32),
                        pltpu.VMEM((tm, pd), BF16)],
        compiler_params=_params(1),
        name="post_mixer",
    )(x, o_prompt, o_sample, w_o, g.reshape(1, d), w_in, w_out, p_prompt, p_sample, w_up,
      g_post.reshape(1, d), g_gate.reshape(1, d), w_gate)


def _headnorm_rope(x, g2, cos, sin_a, sin_b, ones_blk):
    xx = x * x
    hi = xx.astype(BF16)
    lo = (xx - hi.astype(F32)).astype(BF16)
    ss = _dot(hi, ones_blk) + _dot(lo, ones_blk)
    y = x * lax.rsqrt(ss * (1.0 / HEAD_DIM) + EPS) * g2
    half = HEAD_DIM // 8
    return (y * cos + pltpu.roll(y, LANES - half, 1) * sin_a
            + pltpu.roll(y, half, 1) * sin_b)


def _proj_kernel(*refs, segments, n_gains, cache_layout):
    x_ref, g_ref, w_ref = refs[:3]
    gain_refs = refs[3:3 + n_gains]
    cos_ref, sa_ref, sb_ref, ones_ref = refs[3 + n_gains:7 + n_gains]
    out_refs = refs[7 + n_gains:7 + n_gains + len(segments)]
    xn = _rms(x_ref[...], g_ref[...]).astype(BF16)
    y = _dot(xn, w_ref[...])
    cos, sin_a, sin_b, ones_blk = cos_ref[...], sa_ref[...], sb_ref[...], ones_ref[...]
    tm = x_ref.shape[0]
    col = 0
    for s, (o_ref, (n_chunks, gain_idx, scale)) in enumerate(zip(out_refs, segments)):
        for j in range(n_chunks):
            c = y[:, col:col + LANES]
            if gain_idx is not None:
                c = _headnorm_rope(c, gain_refs[gain_idx][...], cos, sin_a, sin_b, ones_blk)
                if scale != 1.0:
                    c = c * scale
            o_ref[:, j * LANES:(j + 1) * LANES] = c
            if cache_layout and s == 1:
                refs[-2][j * LANES:(j + 1) * LANES, :] = c.T
            if cache_layout and s == 2:
                refs[-1][pl.ds(j, tm, stride=n_chunks), :] = c
            col += LANES


def _project(x, g, w, gains, segments, rope, tab_map, tile0, n_tiles, tm, name,
             cache_layout=None):
    d = x.shape[1]
    n = n_tiles * tm
    cos_t, sa_t, sb_t = rope
    ones_blk = (jnp.arange(LANES)[:, None] // HEAD_DIM
                == jnp.arange(LANES)[None, :] // HEAD_DIM).astype(BF16)
    gains2 = [jnp.tile(gv.reshape(1, HEAD_DIM), (1, LANES // HEAD_DIM)) for gv in gains]
    tab_spec = pl.BlockSpec((tm, LANES), tab_map)
    assert sum(s[0] for s in segments) * LANES == w.shape[1]
    out_specs = [pl.BlockSpec((tm, s[0] * LANES), lambda i: (i, 0)) for s in segments]
    out_shape = [jax.ShapeDtypeStruct((n, s[0] * LANES), F32) for s in segments]
    if cache_layout is not None:
        batch, seq = cache_layout
        spt = seq // tm
        kw, vc = segments[1][0] * LANES, segments[2][0]
        assert n == batch * seq
        out_specs += [pl.BlockSpec((kw, tm), lambda i: (i // spt, i % spt)),
                      pl.BlockSpec((tm * vc, LANES), lambda i: (i, 0))]
        out_shape += [jax.ShapeDtypeStruct((batch * kw, seq), F32),
                      jax.ShapeDtypeStruct((n * vc, LANES), F32)]
    return pl.pallas_call(
        functools.partial(_proj_kernel, segments=tuple(segments), n_gains=len(gains),
                          cache_layout=cache_layout is not None),
        grid=(n_tiles,),
        in_specs=[pl.BlockSpec((tm, d), lambda i: (tile0 + i, 0)), _resident((1, d)),
                  _resident(w.shape)]
                 + [_resident((1, LANES))] * len(gains)
                 + [tab_spec, tab_spec, tab_spec, _resident((LANES, LANES))],
        out_specs=out_specs,
        out_shape=out_shape,
        compiler_params=_params(1),
        name=name,
    )(x, g.reshape(1, d), w, *gains2, cos_t, sa_t, sb_t, ones_blk)


def _rope_tables(seq, n_sample_pos, past_len, tm):
    half = HEAD_DIM // 8
    inv_freq = ROPE_THETA ** (-jnp.arange(half, dtype=F32) / half)
    pos = jnp.concatenate([
        jnp.arange(seq, dtype=jnp.int32),
        past_len + (jnp.arange(tm, dtype=jnp.int32) % n_sample_pos)])
    ang = pos.astype(F32)[:, None] * inv_freq[None, :]
    cos, sin = jnp.cos(ang), jnp.sin(ang)
    rows = pos.shape[0]
    pad = jnp.zeros((rows, HEAD_DIM - 2 * half), F32)
    zero = jnp.zeros((rows, half), F32)
    reps = (1, LANES // HEAD_DIM)
    cos_t = jnp.tile(jnp.concatenate([cos, cos, pad + 1.0], axis=1), reps)
    sin_a = jnp.tile(jnp.concatenate([-sin, zero, pad], axis=1), reps)
    sin_b = jnp.tile(jnp.concatenate([zero, sin, pad], axis=1), reps)
    return cos_t, sin_a, sin_b


def _flash_update(s, m_scr):
    m_prev = m_scr[...]
    m_next = jnp.maximum(m_prev, jnp.max(s, axis=1, keepdims=True))
    p = jnp.exp(s - jnp.tile(m_next, (1, s.shape[1] // LANES)))
    alpha = jnp.exp(m_prev - m_next)
    m_scr[...] = m_next
    return p.astype(BF16), jnp.tile(alpha, (1, 2))


def _with_ones(v):
    return jnp.concatenate([v, jnp.ones(v.shape, v.dtype)], axis=1)


def _diff_lambda(lq1_ref, lk1_ref, lq2_ref, lk2_ref, lam_init):
    return (jnp.exp(jnp.sum(lq1_ref[...] * lk1_ref[...], axis=1, keepdims=True))
            - jnp.exp(jnp.sum(lq2_ref[...] * lk2_ref[...], axis=1, keepdims=True))
            + lam_init)


def _sub_norm(o1, o2, lam, g_sub, lam_init):
    d = o1 - lam * o2
    return _rms(d, g_sub) * (1.0 - lam_init)


def _aprompt_kernel(q_ref, k_ref, v_ref, lq1_ref, lk1_ref, lq2_ref, lk2_ref, gsub_ref,
                    o_ref, q4_scr, sa_scr, sb_scr, m_scr, acc_scr, *, blk, lam_init):
    qi = pl.program_id(2)
    lane = lax.broadcasted_iota(jnp.int32, (blk, LANES), 1)
    first = lane < HEAD_DIM
    for g in range(2):
        qg = q_ref[:, g * LANES:(g + 1) * LANES]
        q4_scr[g * blk:(g + 1) * blk, :] = jnp.where(first, qg, 0.0).astype(BF16)
        q4_scr[(2 + g) * blk:(3 + g) * blk, :] = jnp.where(first, 0.0, qg).astype(BF16)
    m_scr[...] = jnp.full(m_scr.shape, NEG_INF, F32)
    acc_scr[...] = jnp.zeros(acc_scr.shape, F32)

    def scores(block):
        kb = k_ref[pl.ds(pl.multiple_of(block * blk, blk), blk), :].astype(BF16)
        return _dot_nt(q4_scr[...], kb)

    def consume(s_ref, block):
        vb = v_ref[pl.ds(pl.multiple_of(block * blk, blk), blk), :].astype(BF16)
        p, alpha = _flash_update(s_ref[...], m_scr)
        acc_scr[...] = alpha * acc_scr[...] + _dot(p, _with_ones(vb))

    s = scores(qi)
    tok = jnp.bitwise_and(lax.broadcasted_iota(jnp.int32, s.shape, 0), blk - 1)
    key = lax.broadcasted_iota(jnp.int32, s.shape, 1)
    sa_scr[...] = jnp.where(key <= tok, s, NEG_INF)

    last = jnp.maximum(qi - 1, 0)

    def body(u, carry):
        sb_scr[...] = scores(jnp.minimum(2 * u, last))
        consume(sa_scr, jnp.where(u == 0, qi, 2 * u - 1))

        @pl.when(2 * u + 1 <= qi)
        def _():
            sa_scr[...] = scores(jnp.minimum(2 * u + 1, last))
            consume(sb_scr, 2 * u)

        return carry

    lax.fori_loop(0, qi // 2 + 1, body, 0)

    o = acc_scr[:, :LANES] / acc_scr[:, LANES:]
    lam = _diff_lambda(lq1_ref, lk1_ref, lq2_ref, lk2_ref, lam_init)
    for g in range(2):
        o1 = o[g * blk:(g + 1) * blk, :]
        o2 = o[(2 + g) * blk:(3 + g) * blk, :]
        o_ref[:, g * LANES:(g + 1) * LANES] = _sub_norm(o1, o2, lam, gsub_ref[...], lam_init)


def _a_prompt_attention(q, k, v, lam_vecs, g_sub, lam_init, batch, seq, blk):
    n_kv = k.shape[1] // LANES
    nq = seq // blk
    vec = lambda a: a.reshape(1, -1)
    return pl.pallas_call(
        functools.partial(_aprompt_kernel, blk=blk, lam_init=lam_init),
        grid=(batch, n_kv, nq),
        in_specs=[pl.BlockSpec((blk, 2 * LANES), lambda b, h, i: (b * nq + i, h)),
                  pl.BlockSpec((seq, LANES), lambda b, h, i: (b, h)),
                  pl.BlockSpec((seq, LANES), lambda b, h, i: (b, h))]
                 + [_resident((1, HEAD_DIM))] * 4 + [_resident((1, LANES))],
        out_specs=pl.BlockSpec((blk, 2 * LANES), lambda b, h, i: (b * nq + i, h)),
        out_shape=jax.ShapeDtypeStruct((batch * seq, 2 * n_kv * LANES), F32),
        scratch_shapes=[pltpu.VMEM((4 * blk, LANES), BF16),
                        pltpu.VMEM((4 * blk, blk), F32),
                        pltpu.VMEM((4 * blk, blk), F32),
                        pltpu.VMEM((4 * blk, LANES), F32),
                        pltpu.VMEM((4 * blk, 2 * LANES), F32)],
        compiler_params=_params(3),
        name="a_prompt_attention",
    )(q, k, v, *[vec(a) for a in lam_vecs], vec(g_sub))


def _asample_kernel(pt_ref, q_ref, kn_ref, vn_ref, lq1_ref, lk1_ref, lq2_ref, lk2_ref,
                    gsub_ref, *rest, n_seq, n_pages_step, n_kv, page, lam_init):
    del pt_ref
    n_win = n_seq * n_pages_step
    kt_pages = rest[:n_win]
    v_pages = rest[n_win:2 * n_win]
    o_ref, qbd_scr, ktbuf, vbuf, m_scr, acc_scr = rest[2 * n_win:]
    j = pl.program_id(1)
    t = q_ref.shape[0] // n_seq
    rows_kv = 4 * t

    @pl.when(j == 0)
    def _():
        vbuf[:, :, :, LANES:] = jnp.ones((n_seq, n_kv, n_pages_step * page, LANES), BF16)
        qbd_scr[...] = jnp.zeros(qbd_scr.shape, F32)
        lane = lax.broadcasted_iota(jnp.int32, (t, LANES), 1)
        first = lane < HEAD_DIM
        for b in range(n_seq):
            for h in range(n_kv):
                for c in range(2):
                    for g in range(2):
                        r0 = ((h * 2 + c) * 2 + g) * t
                        qg = q_ref[b * t:(b + 1) * t, (2 * h + g) * LANES:(2 * h + g + 1) * LANES]
                        keep = first if c == 0 else jnp.logical_not(first)
                        qbd_scr[b, r0:r0 + t, h * LANES:(h + 1) * LANES] = jnp.where(keep, qg, 0.0)
        m_scr[...] = jnp.full(m_scr.shape, NEG_INF, F32)
        acc_scr[...] = jnp.zeros(acc_scr.shape, F32)

    def accumulate(b, s, v_of):
        p, alpha = _flash_update(s, m_scr.at[b])
        for h in range(n_kv):
            r = slice(h * rows_kv, (h + 1) * rows_kv)
            acc_scr[b, r, :] = alpha[r, :] * acc_scr[b, r, :] + _dot(p[r, :], v_of(h))

    qbd = [qbd_scr[b].astype(BF16) for b in range(n_seq)]
    for b in range(n_seq):
        for r in range(n_pages_step):
            w = b * n_pages_step + r
            ktbuf[b, :, r * page:(r + 1) * page] = kt_pages[w][...].astype(BF16)
            for h in range(n_kv):
                vbuf[b, h, r * page:(r + 1) * page, :LANES] = (
                    v_pages[w][pl.ds(h, page, stride=n_kv), :].astype(BF16))
        accumulate(b, _dot(qbd[b], ktbuf[b]), lambda h, b=b: vbuf[b, h])

    @pl.when(j == pl.num_programs(1) - 1)
    def _():
        lam = _diff_lambda(lq1_ref, lk1_ref, lq2_ref, lk2_ref, lam_init)
        pad = jnp.zeros((LANES - t, kn_ref.shape[1]), F32)
        for b in range(n_seq):
            kn = jnp.concatenate([kn_ref[b * t:(b + 1) * t, :], pad], axis=0).astype(BF16)
            vn = jnp.concatenate([vn_ref[b * t:(b + 1) * t, :], pad], axis=0).astype(BF16)
            s = _dot_nt(qbd[b], kn)
            tok = jnp.bitwise_and(lax.broadcasted_iota(jnp.int32, s.shape, 0), t - 1)
            key = lax.broadcasted_iota(jnp.int32, s.shape, 1)
            s = jnp.where(key <= tok, s, NEG_INF)
            accumulate(b, s, lambda h, vn=vn: _with_ones(vn[:, h * LANES:(h + 1) * LANES]))
            o = acc_scr[b, :, :LANES] / acc_scr[b, :, LANES:]
            for h in range(n_kv):
                for g in range(2):
                    r1 = ((h * 2 + 0) * 2 + g) * t
                    r2 = ((h * 2 + 1) * 2 + g) * t
                    o_ref[b * t:(b + 1) * t, (2 * h + g) * LANES:(2 * h + g + 1) * LANES] = (
                        _sub_norm(o[r1:r1 + t, :], o[r2:r2 + t, :], lam, gsub_ref[...], lam_init))


def _a_sample_attention(q, k, v, cache_kt, cache_v, layer, page_table, lam_vecs, g_sub,
                        lam_init, dec_b, dec_t):
    n_pages = page_table.shape[1]
    kw, page = cache_kt.shape[2], cache_kt.shape[3]
    n_kv = kw // LANES
    assert 4 * dec_t * n_kv == LANES and page == LANES
    assert cache_v.shape[2:] == (page * n_kv, LANES)
    n_seq = 2 if dec_b % 2 == 0 else 1
    n_pages_step = math.gcd(n_pages, 16)
    rows = n_seq * dec_t
    vec = lambda a: a.reshape(1, -1)

    def page_spec(b, r):
        return pl.BlockSpec(
            (None, None, kw, page),
            lambda i, j, pt: (layer, pt[i * n_seq + b, j * n_pages_step + r], 0, 0))

    pages = [page_spec(b, r) for b in range(n_seq) for r in range(n_pages_step)]
    tok_map = lambda i, j, pt: (i, 0)
    grid_spec = pltpu.PrefetchScalarGridSpec(
        num_scalar_prefetch=1,
        grid=(dec_b // n_seq, n_pages // n_pages_step),
        in_specs=[pl.BlockSpec((rows, q.shape[1]), tok_map),
                  pl.BlockSpec((rows, kw), tok_map),
                  pl.BlockSpec((rows, kw), tok_map)]
                 + [pl.BlockSpec((1, HEAD_DIM), lambda i, j, pt: (0, 0))] * 4
                 + [pl.BlockSpec((1, LANES), lambda i, j, pt: (0, 0))]
                 + pages * 2,
        out_specs=pl.BlockSpec((rows, q.shape[1]), tok_map),
        scratch_shapes=[pltpu.VMEM((n_seq, LANES, kw), F32),
                        pltpu.VMEM((n_seq, kw, n_pages_step * page), BF16),
                        pltpu.VMEM((n_seq, n_kv, n_pages_step * page, 2 * LANES), BF16),
                        pltpu.VMEM((n_seq, LANES, LANES), F32),
                        pltpu.VMEM((n_seq, LANES, 2 * LANES), F32)])
    n_win = n_seq * n_pages_step
    return pl.pallas_call(
        functools.partial(_asample_kernel, n_seq=n_seq, n_pages_step=n_pages_step, n_kv=n_kv,
                          page=page, lam_init=lam_init),
        grid_spec=grid_spec,
        out_shape=jax.ShapeDtypeStruct((dec_b * dec_t, q.shape[1]), F32),
        compiler_params=_params(2),
        name="a_sample_attention",
    )(page_table, q, k, v, *[vec(a) for a in lam_vecs], vec(g_sub),
      *([cache_kt] * n_win), *([cache_v] * n_win))


def _dup_head(x, h):
    lane = lax.broadcasted_iota(jnp.int32, x.shape, 1)
    keep = (lane < HEAD_DIM) if h == 0 else (lane >= HEAD_DIM)
    return jnp.where(keep, x, pltpu.roll(x, HEAD_DIM, 1))


def _swa_heads(q_ref, k2, v2, sink_ref, o_ref, valid, t, n_group):
    lane = lax.broadcasted_iota(jnp.int32, (t, LANES), 1)
    first = lane < HEAD_DIM
    n_kv = k2.shape[1] // HEAD_DIM
    pairs = n_group // 2
    for h in range(n_kv):
        kd = _dup_head(k2, h).astype(BF16)
        vd = _dup_head(v2, h).astype(BF16)
        parts = []
        for r in range(n_group):
            c = h * pairs + r // 2
            q2 = q_ref[:, c * LANES:(c + 1) * LANES]
            parts.append(jnp.where(first if r % 2 == 0 else jnp.logical_not(first), q2, 0.0))
        qs = jnp.concatenate(parts, axis=0).astype(BF16)
        s = _dot_nt(qs, kd)
        ps = []
        for r in range(n_group):
            sr = jnp.where(valid, s[r * t:(r + 1) * t, :], NEG_INF)
            sink = sink_ref[h * n_group + r]
            m = jnp.maximum(jnp.max(sr, axis=1, keepdims=True), sink)
            p = jnp.exp(sr - m)
            l = jnp.sum(p, axis=1, keepdims=True) + jnp.exp(sink - m)
            ps.append(p / l)
        o = _dot(jnp.concatenate(ps, axis=0).astype(BF16), vd)
        for pr in range(pairs):
            c = h * pairs + pr
            o_ref[:, c * LANES:(c + 1) * LANES] = jnp.where(
                first, o[(2 * pr) * t:(2 * pr + 1) * t, :], o[(2 * pr + 1) * t:(2 * pr + 2) * t, :])


def _bprompt_kernel(q_ref, kp_ref, kc_ref, vp_ref, vc_ref, sink_ref, o_ref, *, n_group):
    i = pl.program_id(1)
    t = q_ref.shape[0]
    k2 = jnp.concatenate([kp_ref[...], kc_ref[...]], axis=0)
    v2 = jnp.concatenate([vp_ref[...], vc_ref[...]], axis=0)
    tok = lax.broadcasted_iota(jnp.int32, (t, 2 * t), 0)
    key = lax.broadcasted_iota(jnp.int32, (t, 2 * t), 1)
    valid = (key > tok) & (key <= tok + t) & (key >= jnp.where(i > 0, 0, t))
    _swa_heads(q_ref, k2, v2, sink_ref, o_ref, valid, t, n_group)


def _b_prompt_attention(q, kb, vb, sinks, batch, seq):
    t = WINDOW
    nb = seq // t
    n_group = (q.shape[1] // HEAD_DIM) // (kb.shape[1] // HEAD_DIM)
    cur = lambda b, i: (b * nb + i, 0)
    prev = lambda b, i: (b * nb + jnp.maximum(i - 1, 0), 0)
    return pl.pallas_call(
        functools.partial(_bprompt_kernel, n_group=n_group),
        grid=(batch, nb),
        in_specs=[pl.BlockSpec((t, q.shape[1]), cur),
                  pl.BlockSpec((t, LANES), prev), pl.BlockSpec((t, LANES), cur),
                  pl.BlockSpec((t, LANES), prev), pl.BlockSpec((t, LANES), cur),
                  pl.BlockSpec(memory_space=pltpu.SMEM)],
        out_specs=pl.BlockSpec((t, q.shape[1]), cur),
        out_shape=jax.ShapeDtypeStruct((batch * seq, q.shape[1]), F32),
        compiler_params=_params(2),
        name="b_prompt_attention",
    )(q, kb, kb, vb, vb, sinks)


def _bsample_kernel(q_ref, kw_ref, kn_ref, vw_ref, vn_ref, sink_ref, o_ref, *, n_group):
    t = q_ref.shape[0]
    w = kw_ref.shape[0]
    pad = jnp.zeros((w - t, LANES), F32)
    k2 = jnp.concatenate([kw_ref[...], kn_ref[...], pad], axis=0)
    v2 = jnp.concatenate([vw_ref[...], vn_ref[...], pad], axis=0)
    tok = lax.broadcasted_iota(jnp.int32, (t, 2 * w), 0)
    key = lax.broadcasted_iota(jnp.int32, (t, 2 * w), 1)
    valid = (key > tok + (w - WINDOW)) & (key <= tok + w)
    _swa_heads(q_ref, k2, v2, sink_ref, o_ref, valid, t, n_group)


def _b_sample_attention(q, kb, vb, cache_k_win, cache_v_win, sinks, n_prompt, dec_b, dec_t):
    w = cache_k_win.shape[1]
    row0 = n_prompt // dec_t
    n_group = (q.shape[1] // HEAD_DIM) // (kb.shape[1] // HEAD_DIM)
    tok = lambda b: (row0 + b, 0)
    win = pl.BlockSpec((None, w, LANES), lambda b: (b, 0, 0))
    return pl.pallas_call(
        functools.partial(_bsample_kernel, n_group=n_group),
        grid=(dec_b,),
        in_specs=[pl.BlockSpec((dec_t, q.shape[1]), tok),
                  win, pl.BlockSpec((dec_t, LANES), tok),
                  win, pl.BlockSpec((dec_t, LANES), tok),
                  pl.BlockSpec(memory_space=pltpu.SMEM)],
        out_specs=pl.BlockSpec((dec_t, q.shape[1]), lambda b: (b, 0)),
        out_shape=jax.ShapeDtypeStruct((dec_b * dec_t, q.shape[1]), F32),
        compiler_params=_params(1),
        name="b_sample_attention",
    )(q, cache_k_win.reshape(dec_b, w, LANES), kb, cache_v_win.reshape(dec_b, w, LANES), vb,
      sinks)


def _token_tile(seq, n_sample):
    for tm in (512, 256, 128, 64, 32, 16, 8):
        if seq % tm == 0 and n_sample % tm == 0:
            return tm
    raise ValueError("unsupported token counts")


def kernel(x_prompt, x_sample, cache_k_a, cache_v_a, cache_k_win, cache_v_win, page_table, p_prompt, p_sample, g_ffn1, w_ffn1_in, w_ffn1_out, g_mix, g_ffn2, w_ffn2_in, w_ffn2_out, w_qkv_a, g_q_a, g_k_a, lambda_q1, lambda_k1, lambda_q2, lambda_k2, g_sub_a, w_o_a, g_kv, w_kv, g_k_b, w_q_b, g_q_b, sinks_b, w_o_b, w_ple_up, g_ple_post, g_ple_gate, w_ple_gate):
    batch, seq, d = x_prompt.shape
    dec_b, dec_t, _ = x_sample.shape
    depth = g_ffn1.shape[0]
    n_a = w_qkv_a.shape[0]
    n_pool, page = cache_k_a.shape[1], cache_k_a.shape[2]
    a_kv = cache_k_a.shape[3]
    past_len = page_table.shape[1] * page
    w_buf = cache_k_win.shape[1]
    n_prompt = batch * seq
    n_sample = dec_b * dec_t
    tm = _token_tile(seq, n_sample)
    assert tm % dec_t == 0 and seq % WINDOW == 0 and w_buf == WINDOW

    bf = lambda w: w.astype(BF16)
    p_prompt = p_prompt.reshape(depth, n_prompt, -1)
    p_sample = p_sample.reshape(depth, n_sample, -1)
    cache_kt = jnp.transpose(cache_k_a, (0, 1, 3, 4, 5, 2)).reshape(n_a, n_pool, -1, page)
    cache_v = cache_v_a.reshape(n_a, n_pool, page * a_kv, -1)

    n_pt, spt, n_st = n_prompt // tm, seq // tm, n_sample // tm
    rope = _rope_tables(seq, dec_t, past_len, tm)
    prompt_tab = lambda i: (i % spt, 0)
    sample_tab = lambda i: (spt, 0)
    all_tab = lambda i: (jnp.where(i < n_pt, i % spt, spt), 0)
    attn_blk = next(b for b in (512, 256, WINDOW) if seq % b == 0)
    kv_chunks = a_kv * 2 * HEAD_DIM // LANES
    a_segments = [(2 * kv_chunks, 0, QK_SCALE), (kv_chunks, 1, 1.0), (kv_chunks, None, 1.0)]

    ka_p, va_p, ka_s, va_s = [], [], [], []
    kb = vb = None
    for i in range(depth):
        if i == n_a:
            kb, vb = _project(x, g_kv, bf(w_kv), [g_k_b], [(1, 0, 1.0), (1, None, 1.0)],
                              rope, all_tab, 0, n_pt + n_st, tm, "shared_kv_proj")
        if i == 0:
            x = _ffn_half_join(x_prompt.reshape(n_prompt, d), x_sample.reshape(n_sample, d),
                               g_ffn1[i], bf(w_ffn1_in[i]), bf(w_ffn1_out[i]), tm)
        else:
            x = _ffn_half(x, g_ffn1[i], bf(w_ffn1_in[i]), bf(w_ffn1_out[i]), tm)
        if i < n_a:
            lam_init = 0.8 - 0.6 * math.exp(-0.3 * i)
            lam_vecs = (lambda_q1[i], lambda_k1[i], lambda_q2[i], lambda_k2[i])
            w_qkv, gains = bf(w_qkv_a[i]), [g_q_a[i], g_k_a[i]]
            q, k, v, k_cache, v_cache = _project(
                x, g_mix[i], w_qkv, gains, a_segments, rope, prompt_tab, 0, n_pt, tm,
                "a_qkv_proj_prompt", cache_layout=(batch, seq))
            q_s, k_s, v_s = _project(
                x, g_mix[i], w_qkv, gains, a_segments, rope, sample_tab, n_pt, n_st, tm,
                "a_qkv_proj_sample")
            ka_p.append(k_cache)
            va_p.append(v_cache)
            ka_s.append(k_s)
            va_s.append(v_s)
            o_p = _a_prompt_attention(q, k, v, lam_vecs, g_sub_a[i], lam_init, batch, seq,
                                      attn_blk)
            o_s = _a_sample_attention(q_s, k_s, v_s, cache_kt, cache_v, i, page_table,
                                      lam_vecs, g_sub_a[i], lam_init, dec_b, dec_t)
            w_o = bf(w_o_a[i])
        else:
            j = i - n_a
            (q,) = _project(x, g_mix[i], bf(w_q_b[j]), [g_q_b[j]],
                            [(w_q_b.shape[2] // LANES, 0, QK_SCALE)], rope, all_tab, 0,
                            n_pt + n_st, tm, "b_q_proj")
            o_p = _b_prompt_attention(q, kb, vb, sinks_b[j], batch, seq)
            o_s = _b_sample_attention(q, kb, vb, cache_k_win, cache_v_win, sinks_b[j],
                                      n_prompt, dec_b, dec_t)
            w_o = bf(w_o_b[j])
        ys = _post_mixer(x, o_p, o_s, w_o, g_ffn2[i], bf(w_ffn2_in[i]), bf(w_ffn2_out[i]),
                         p_prompt, p_sample, i, bf(w_ple_up[i]), g_ple_post[i], g_ple_gate[i],
                         bf(w_ple_gate[i]), tm, split=(i == depth - 1))
        x = ys[0]
    y_prompt, y_sample = ys

    ka_p = jnp.transpose(jnp.stack(ka_p).reshape(n_a, batch, a_kv, 2, HEAD_DIM, seq),
                         (0, 1, 5, 2, 3, 4))
    va_p = jnp.stack(va_p).reshape(n_a, batch, seq, a_kv, 2 * HEAD_DIM)
    ka_s = jnp.stack(ka_s).reshape(n_a, dec_b, dec_t, a_kv, 2, HEAD_DIM)
    va_s = jnp.stack(va_s).reshape(n_a, dec_b, dec_t, a_kv, 2 * HEAD_DIM)
    b_kv = kb.shape[1] // HEAD_DIM
    kb_p = kb[:n_prompt].reshape(batch, seq, b_kv, HEAD_DIM)
    vb_p = vb[:n_prompt].reshape(batch, seq, b_kv, HEAD_DIM)
    kb_s = kb[n_prompt:].reshape(dec_b, dec_t, b_kv, HEAD_DIM)
    vb_s = vb[n_prompt:].reshape(dec_b, dec_t, b_kv, HEAD_DIM)
    kw_s = jnp.concatenate([cache_k_win, kb_s], axis=1)
    vw_s = jnp.concatenate([cache_v_win, vb_s], axis=1)
    return (y_prompt.reshape(batch, seq, d), y_sample.reshape(dec_b, dec_t, d),
            ka_p, va_p, ka_s, va_s,
            kb_p[:, -w_buf:], vb_p[:, -w_buf:], kw_s[:, -w_buf:], vw_s[:, -w_buf:])
```

```python
import functools
import math

import jax
import jax.numpy as jnp
from jax import lax
from jax.experimental import pallas as pl
from jax.experimental.pallas import tpu as pltpu

F32 = jnp.float32
BF16 = jnp.bfloat16

HEAD_DIM = 64
WINDOW = 128
ROPE_THETA = 500000.0
EPS = 1e-6
LANES = 128
QK_SCALE = HEAD_DIM ** -0.5
VMEM_LIMIT_BYTES = 56 * 1024 * 1024
NEG_INF = float("-inf")


def _params(n_axes):
    return pltpu.CompilerParams(
        dimension_semantics=("arbitrary",) * n_axes,
        vmem_limit_bytes=VMEM_LIMIT_BYTES)


def _resident(shape):
    nd = len(shape)
    return pl.BlockSpec(shape, lambda *_: (0,) * nd, pipeline_mode=pl.Buffered(1))


class _LayerWeight:
    def __init__(self, stack, layer):
        self.stack, self.layer = stack, layer
        self.shape = stack.shape[1:]

    def spec(self):
        layer = self.layer
        return pl.BlockSpec((None,) + self.shape, lambda *_: (layer, 0, 0),
                            pipeline_mode=pl.Buffered(1))


def _rms(x, g):
    ms = jnp.mean(x * x, axis=-1, keepdims=True)
    return x * lax.rsqrt(ms + EPS) * g


def _dot(a, b):
    return jnp.dot(a, b, preferred_element_type=F32)


def _dot_nt(a, b):
    return lax.dot_general(a, b, (((1,), (1,)), ((), ())), preferred_element_type=F32)


FFN_CHUNK = 256


def _ffn_body(x, g_ref, win_ref, wout_ref, h_scr):
    d_ff = wout_ref.shape[0]
    xn = _rms(x, g_ref[...]).astype(BF16)
    for c in range(d_ff // FFN_CHUNK):
        lo, hi = c * FFN_CHUNK, (c + 1) * FFN_CHUNK
        a = _dot(xn, win_ref[:, lo:hi])
        b = _dot(xn, win_ref[:, d_ff + lo:d_ff + hi])
        h_scr[:, lo:hi] = (a * jax.nn.sigmoid(a) * b).astype(BF16)
    return x + 0.5 * _dot(h_scr[...], wout_ref[...])


def _ffn_kernel(x_ref, g_ref, win_ref, wout_ref, o_ref, h_scr):
    o_ref[...] = _ffn_body(x_ref[...], g_ref, win_ref, wout_ref, h_scr)


def _ffn_join_kernel(xp_ref, xs_ref, g_ref, win_ref, wout_ref, o_ref, h_scr, x_scr, *,
                     n_prompt_tiles):
    i = pl.program_id(0)

    @pl.when(i < n_prompt_tiles)
    def _():
        x_scr[...] = xp_ref[...]

    @pl.when(i >= n_prompt_tiles)
    def _():
        x_scr[...] = xs_ref[...]

    o_ref[...] = _ffn_body(x_scr[...], g_ref, win_ref, wout_ref, h_scr)


def _ffn_half_join(x_prompt, x_sample, g, w_in, w_out, tm):
    d = x_prompt.shape[1]
    d_ff = w_out.shape[0]
    npt, nst = x_prompt.shape[0] // tm, x_sample.shape[0] // tm
    assert x_prompt.shape[0] % tm == 0 and x_sample.shape[0] % tm == 0
    assert d_ff % FFN_CHUNK == 0
    return pl.pallas_call(
        functools.partial(_ffn_join_kernel, n_prompt_tiles=npt),
        grid=(npt + nst,),
        in_specs=[pl.BlockSpec((tm, d), lambda i: (jnp.minimum(i, npt - 1), 0)),
                  pl.BlockSpec((tm, d), lambda i: (jnp.maximum(i - npt, 0), 0)),
                  _resident((1, d)), w_in.spec(), w_out.spec()],
        out_specs=pl.BlockSpec((tm, d), lambda i: (i, 0)),
        out_shape=jax.ShapeDtypeStruct(((npt + nst) * tm, d), F32),
        scratch_shapes=[pltpu.VMEM((tm, d_ff), BF16), pltpu.VMEM((tm, d), F32)],
        compiler_params=_params(1),
        name="ffn_half_join",
    )(x_prompt, x_sample, g.reshape(1, d), w_in.stack, w_out.stack)


def _ffn_half(x, g, w_in, w_out, tm):
    n, d = x.shape
    d_ff = w_out.shape[0]
    assert d_ff % FFN_CHUNK == 0 and n % tm == 0
    return pl.pallas_call(
        _ffn_kernel,
        grid=(n // tm,),
        in_specs=[pl.BlockSpec((tm, d), lambda i: (i, 0)),
                  _resident((1, d)), w_in.spec(), w_out.spec()],
        out_specs=pl.BlockSpec((tm, d), lambda i: (i, 0)),
        out_shape=jax.ShapeDtypeStruct((n, d), F32),
        scratch_shapes=[pltpu.VMEM((tm, d_ff), BF16)],
        compiler_params=_params(1),
        name="ffn_half",
    )(x, g.reshape(1, d), w_in.stack, w_out.stack)


def _post_mixer_kernel(x_ref, op_ref, os_ref, wo_ref, g_ref, win_ref, wout_ref, pp_ref, ps_ref,
                       wup_ref, gpost_ref, ggate_ref, wgate_ref, *rest, n_prompt_tiles, split):
    y_refs, (h_scr, x1_scr, p_scr) = rest[:-3], rest[-3:]
    i = pl.program_id(0)

    @pl.when(i < n_prompt_tiles)
    def _():
        x1_scr[...] = x_ref[...] + _dot(op_ref[...].astype(BF16), wo_ref[...])
        p_scr[...] = pp_ref[...].astype(BF16)

    @pl.when(i >= n_prompt_tiles)
    def _():
        x1_scr[...] = x_ref[...] + _dot(os_ref[...].astype(BF16), wo_ref[...])
        p_scr[...] = ps_ref[...].astype(BF16)

    x2 = _ffn_body(x1_scr[...], g_ref, win_ref, wout_ref, h_scr)
    e = _rms(_dot(p_scr[...], wup_ref[...]), gpost_ref[...])
    gate = jax.nn.sigmoid(_dot(_rms(x2, ggate_ref[...]).astype(BF16), wgate_ref[...]))
    y = x2 + gate * e
    if not split:
        y_refs[0][...] = y
    else:
        @pl.when(i < n_prompt_tiles)
        def _():
            y_refs[0][...] = y

        @pl.when(i >= n_prompt_tiles)
        def _():
            y_refs[1][...] = y


def _post_mixer(x, o_prompt, o_sample, w_o, g, w_in, w_out, p_prompt, p_sample, layer, w_up,
                g_post, g_gate, w_gate, tm, split):
    n, d = x.shape
    d_ff = w_out.shape[0]
    pd = p_prompt.shape[2]
    npt, nst = o_prompt.shape[0] // tm, o_sample.shape[0] // tm
    assert o_prompt.shape[0] % tm == 0 and o_sample.shape[0] % tm == 0
    assert d_ff % FFN_CHUNK == 0 and n == (npt + nst) * tm
    row = lambda i: (i, 0)
    prompt_row = lambda i: (jnp.minimum(i, npt - 1), 0)
    sample_row = lambda i: (jnp.maximum(i - npt, 0), 0)
    if split:
        out_specs = [pl.BlockSpec((tm, d), prompt_row), pl.BlockSpec((tm, d), sample_row)]
        out_shape = [jax.ShapeDtypeStruct((npt * tm, d), F32),
                     jax.ShapeDtypeStruct((nst * tm, d), F32)]
    else:
        out_specs = [pl.BlockSpec((tm, d), row)]
        out_shape = [jax.ShapeDtypeStruct((n, d), F32)]
    return pl.pallas_call(
        functools.partial(_post_mixer_kernel, n_prompt_tiles=npt, split=split),
        grid=(n // tm,),
        in_specs=[pl.BlockSpec((tm, d), row),
                  pl.BlockSpec((tm, o_prompt.shape[1]), prompt_row),
                  pl.BlockSpec((tm, o_sample.shape[1]), sample_row),
                  w_o.spec(), _resident((1, d)), w_in.spec(), w_out.spec(),
                  pl.BlockSpec((None, tm, pd), lambda i: (layer, jnp.minimum(i, npt - 1), 0)),
                  pl.BlockSpec((None, tm, pd), lambda i: (layer, jnp.maximum(i - npt, 0), 0)),
                  w_up.spec(), _resident((1, d)), _resident((1, d)), w_gate.spec()],
        out_specs=out_specs,
        out_shape=out_shape,
        scratch_shapes=[pltpu.VMEM((tm, d_ff), BF16), pltpu.VMEM((tm, d), F32),
                        pltpu.VMEM((tm, pd), BF16)],
        compiler_params=_params(1),
        name="post_mixer",
    )(x, o_prompt, o_sample, w_o.stack, g.reshape(1, d), w_in.stack, w_out.stack, p_prompt,
      p_sample, w_up.stack, g_post.reshape(1, d), g_gate.reshape(1, d), w_gate.stack)


def _headnorm_rope(x, g2, cos, sin_a, sin_b, ones_blk):
    xx = x * x
    hi = xx.astype(BF16)
    lo = (xx - hi.astype(F32)).astype(BF16)
    ss = _dot(hi, ones_blk) + _dot(lo, ones_blk)
    y = x * lax.rsqrt(ss * (1.0 / HEAD_DIM) + EPS) * g2
    half = HEAD_DIM // 8
    return (y * cos + pltpu.roll(y, LANES - half, 1) * sin_a
            + pltpu.roll(y, half, 1) * sin_b)


def _proj_kernel(*refs, segments, n_gains, cache_layout):
    x_ref, g_ref, w_ref = refs[:3]
    gain_refs = refs[3:3 + n_gains]
    cos_ref, sa_ref, sb_ref, ones_ref = refs[3 + n_gains:7 + n_gains]
    out_refs = refs[7 + n_gains:7 + n_gains + len(segments)]
    xn = _rms(x_ref[...], g_ref[...]).astype(BF16)
    y = _dot(xn, w_ref[...])
    cos, sin_a, sin_b, ones_blk = cos_ref[...], sa_ref[...], sb_ref[...], ones_ref[...]
    tm = x_ref.shape[0]
    col = 0
    for s, (o_ref, (n_chunks, gain_idx, scale)) in enumerate(zip(out_refs, segments)):
        for j in range(n_chunks):
            c = y[:, col:col + LANES]
            if gain_idx is not None:
                c = _headnorm_rope(c, gain_refs[gain_idx][...], cos, sin_a, sin_b, ones_blk)
                if scale != 1.0:
                    c = c * scale
            o_ref[:, j * LANES:(j + 1) * LANES] = c
            if cache_layout and s == 1:
                refs[-2][j * LANES:(j + 1) * LANES, :] = c.T
            if cache_layout and s == 2:
                refs[-1][pl.ds(j, tm, stride=n_chunks), :] = c
            col += LANES


def _project(x, g, w, gains, segments, rope, tab_map, tile0, n_tiles, tm, name,
             cache_layout=None):
    d = x.shape[1]
    n = n_tiles * tm
    cos_t, sa_t, sb_t = rope
    ones_blk = (jnp.arange(LANES)[:, None] // HEAD_DIM
                == jnp.arange(LANES)[None, :] // HEAD_DIM).astype(BF16)
    gains2 = [jnp.tile(gv.reshape(1, HEAD_DIM), (1, LANES // HEAD_DIM)) for gv in gains]
    tab_spec = pl.BlockSpec((tm, LANES), tab_map)
    assert sum(s[0] for s in segments) * LANES == w.shape[1]
    out_specs = [pl.BlockSpec((tm, s[0] * LANES), lambda i: (i, 0)) for s in segments]
    out_shape = [jax.ShapeDtypeStruct((n, s[0] * LANES), F32) for s in segments]
    if cache_layout is not None:
        batch, seq = cache_layout
        spt = seq // tm
        kw, vc = segments[1][0] * LANES, segments[2][0]
        assert n == batch * seq
        out_specs += [pl.BlockSpec((kw, tm), lambda i: (i // spt, i % spt)),
                      pl.BlockSpec((tm * vc, LANES), lambda i: (i, 0))]
        out_shape += [jax.ShapeDtypeStruct((batch * kw, seq), F32),
                      jax.ShapeDtypeStruct((n * vc, LANES), F32)]
    return pl.pallas_call(
        functools.partial(_proj_kernel, segments=tuple(segments), n_gains=len(gains),
                          cache_layout=cache_layout is not None),
        grid=(n_tiles,),
        in_specs=[pl.BlockSpec((tm, d), lambda i: (tile0 + i, 0)), _resident((1, d)),
                  w.spec()]
                 + [_resident((1, LANES))] * len(gains)
                 + [tab_spec, tab_spec, tab_spec, _resident((LANES, LANES))],
        out_specs=out_specs,
        out_shape=out_shape,
        compiler_params=_params(1),
        name=name,
    )(x, g.reshape(1, d), w.stack, *gains2, cos_t, sa_t, sb_t, ones_blk)


def _rope_tables(seq, n_sample_pos, past_len, tm):
    half = HEAD_DIM // 8
    inv_freq = ROPE_THETA ** (-jnp.arange(half, dtype=F32) / half)
    pos = jnp.concatenate([
        jnp.arange(seq, dtype=jnp.int32),
        past_len + (jnp.arange(tm, dtype=jnp.int32) % n_sample_pos)])
    ang = pos.astype(F32)[:, None] * inv_freq[None, :]
    cos, sin = jnp.cos(ang), jnp.sin(ang)
    rows = pos.shape[0]
    pad = jnp.zeros((rows, HEAD_DIM - 2 * half), F32)
    zero = jnp.zeros((rows, half), F32)
    reps = (1, LANES // HEAD_DIM)
    cos_t = jnp.tile(jnp.concatenate([cos, cos, pad + 1.0], axis=1), reps)
    sin_a = jnp.tile(jnp.concatenate([-sin, zero, pad], axis=1), reps)
    sin_b = jnp.tile(jnp.concatenate([zero, sin, pad], axis=1), reps)
    return cos_t, sin_a, sin_b


def _flash_update(s, m_scr):
    m_prev = m_scr[...]
    m_next = jnp.maximum(m_prev, jnp.max(s, axis=1, keepdims=True))
    p = jnp.exp(s - jnp.tile(m_next, (1, s.shape[1] // LANES)))
    alpha = jnp.exp(m_prev - m_next)
    m_scr[...] = m_next
    return p.astype(BF16), jnp.tile(alpha, (1, 2))


def _with_ones(v):
    return jnp.concatenate([v, jnp.ones(v.shape, v.dtype)], axis=1)


def _diff_lambda(lq1_ref, lk1_ref, lq2_ref, lk2_ref, lam_init):
    return (jnp.exp(jnp.sum(lq1_ref[...] * lk1_ref[...], axis=1, keepdims=True))
            - jnp.exp(jnp.sum(lq2_ref[...] * lk2_ref[...], axis=1, keepdims=True))
            + lam_init)


def _sub_norm(o1, o2, lam, g_sub, lam_init):
    d = o1 - lam * o2
    return _rms(d, g_sub) * (1.0 - lam_init)


def _aprompt_kernel(q_ref, k_ref, v_ref, lq1_ref, lk1_ref, lq2_ref, lk2_ref, gsub_ref,
                    o_ref, q4_scr, sa_scr, sb_scr, m_scr, acc_scr, *, blk, lam_init):
    qi = pl.program_id(2)
    lane = lax.broadcasted_iota(jnp.int32, (blk, LANES), 1)
    first = lane < HEAD_DIM
    for g in range(2):
        qg = q_ref[:, g * LANES:(g + 1) * LANES]
        q4_scr[g * blk:(g + 1) * blk, :] = jnp.where(first, qg, 0.0).astype(BF16)
        q4_scr[(2 + g) * blk:(3 + g) * blk, :] = jnp.where(first, 0.0, qg).astype(BF16)
    m_scr[...] = jnp.full(m_scr.shape, NEG_INF, F32)
    acc_scr[...] = jnp.zeros(acc_scr.shape, F32)

    def scores(block):
        kb = k_ref[pl.ds(pl.multiple_of(block * blk, blk), blk), :].astype(BF16)
        return _dot_nt(q4_scr[...], kb)

    def consume(s_ref, block):
        vb = v_ref[pl.ds(pl.multiple_of(block * blk, blk), blk), :].astype(BF16)
        p, alpha = _flash_update(s_ref[...], m_scr)
        acc_scr[...] = alpha * acc_scr[...] + _dot(p, _with_ones(vb))

    s = scores(qi)
    tok = jnp.bitwise_and(lax.broadcasted_iota(jnp.int32, s.shape, 0), blk - 1)
    key = lax.broadcasted_iota(jnp.int32, s.shape, 1)
    sa_scr[...] = jnp.where(key <= tok, s, NEG_INF)

    last = jnp.maximum(qi - 1, 0)

    def body(u, carry):
        sb_scr[...] = scores(jnp.minimum(2 * u, last))
        consume(sa_scr, jnp.where(u == 0, qi, 2 * u - 1))

        @pl.when(2 * u + 1 <= qi)
        def _():
            sa_scr[...] = scores(jnp.minimum(2 * u + 1, last))
            consume(sb_scr, 2 * u)

        return carry

    lax.fori_loop(0, qi // 2 + 1, body, 0)

    o = acc_scr[:, :LANES] / acc_scr[:, LANES:]
    lam = _diff_lambda(lq1_ref, lk1_ref, lq2_ref, lk2_ref, lam_init)
    for g in range(2):
        o1 = o[g * blk:(g + 1) * blk, :]
        o2 = o[(2 + g) * blk:(3 + g) * blk, :]
        o_ref[:, g * LANES:(g + 1) * LANES] = _sub_norm(o1, o2, lam, gsub_ref[...], lam_init)


def _a_prompt_attention(q, k, v, lam_vecs, g_sub, lam_init, batch, seq, blk):
    n_kv = k.shape[1] // LANES
    nq = seq // blk
    vec = lambda a: a.reshape(1, -1)
    return pl.pallas_call(
        functools.partial(_aprompt_kernel, blk=blk, lam_init=lam_init),
        grid=(batch, n_kv, nq),
        in_specs=[pl.BlockSpec((blk, 2 * LANES), lambda b, h, i: (b * nq + i, h)),
                  pl.BlockSpec((seq, LANES), lambda b, h, i: (b, h)),
                  pl.BlockSpec((seq, LANES), lambda b, h, i: (b, h))]
                 + [_resident((1, HEAD_DIM))] * 4 + [_resident((1, LANES))],
        out_specs=pl.BlockSpec((blk, 2 * LANES), lambda b, h, i: (b * nq + i, h)),
        out_shape=jax.ShapeDtypeStruct((batch * seq, 2 * n_kv * LANES), F32),
        scratch_shapes=[pltpu.VMEM((4 * blk, LANES), BF16),
                        pltpu.VMEM((4 * blk, blk), F32),
                        pltpu.VMEM((4 * blk, blk), F32),
                        pltpu.VMEM((4 * blk, LANES), F32),
                        pltpu.VMEM((4 * blk, 2 * LANES), F32)],
        compiler_params=_params(3),
        name="a_prompt_attention",
    )(q, k, v, *[vec(a) for a in lam_vecs], vec(g_sub))


def _asample_kernel(pt_ref, q_ref, kn_ref, vn_ref, lq1_ref, lk1_ref, lq2_ref, lk2_ref,
                    gsub_ref, *rest, n_seq, n_pages_step, n_kv, page, lam_init):
    del pt_ref
    n_win = n_seq * n_pages_step
    kt_pages = rest[:n_win]
    v_pages = rest[n_win:2 * n_win]
    o_ref, qbd_scr, ktbuf, vbuf, m_scr, acc_scr = rest[2 * n_win:]
    j = pl.program_id(1)
    t = q_ref.shape[0] // n_seq
    rows_kv = 4 * t

    @pl.when(j == 0)
    def _():
        vbuf[:, :, :, LANES:] = jnp.ones((n_seq, n_kv, n_pages_step * page, LANES), BF16)
        qbd_scr[...] = jnp.zeros(qbd_scr.shape, F32)
        lane = lax.broadcasted_iota(jnp.int32, (t, LANES), 1)
        first = lane < HEAD_DIM
        for b in range(n_seq):
            for h in range(n_kv):
                for c in range(2):
                    for g in range(2):
                        r0 = ((h * 2 + c) * 2 + g) * t
                        qg = q_ref[b * t:(b + 1) * t, (2 * h + g) * LANES:(2 * h + g + 1) * LANES]
                        keep = first if c == 0 else jnp.logical_not(first)
                        qbd_scr[b, r0:r0 + t, h * LANES:(h + 1) * LANES] = jnp.where(keep, qg, 0.0)
        m_scr[...] = jnp.full(m_scr.shape, NEG_INF, F32)
        acc_scr[...] = jnp.zeros(acc_scr.shape, F32)

    def accumulate(b, s, v_of):
        p, alpha = _flash_update(s, m_scr.at[b])
        for h in range(n_kv):
            r = slice(h * rows_kv, (h + 1) * rows_kv)
            acc_scr[b, r, :] = alpha[r, :] * acc_scr[b, r, :] + _dot(p[r, :], v_of(h))

    qbd = [qbd_scr[b].astype(BF16) for b in range(n_seq)]
    for b in range(n_seq):
        for r in range(n_pages_step):
            w = b * n_pages_step + r
            ktbuf[b, :, r * page:(r + 1) * page] = kt_pages[w][...].astype(BF16)
            for h in range(n_kv):
                vbuf[b, h, r * page:(r + 1) * page, :LANES] = (
                    v_pages[w][pl.ds(h, page, stride=n_kv), :].astype(BF16))
        accumulate(b, _dot(qbd[b], ktbuf[b]), lambda h, b=b: vbuf[b, h])

    @pl.when(j == pl.num_programs(1) - 1)
    def _():
        lam = _diff_lambda(lq1_ref, lk1_ref, lq2_ref, lk2_ref, lam_init)
        pad = jnp.zeros((LANES - t, kn_ref.shape[1]), F32)
        for b in range(n_seq):
            kn = jnp.concatenate([kn_ref[b * t:(b + 1) * t, :], pad], axis=0).astype(BF16)
            vn = jnp.concatenate([vn_ref[b * t:(b + 1) * t, :], pad], axis=0).astype(BF16)
            s = _dot_nt(qbd[b], kn)
            tok = jnp.bitwise_and(lax.broadcasted_iota(jnp.int32, s.shape, 0), t - 1)
            key = lax.broadcasted_iota(jnp.int32, s.shape, 1)
            s = jnp.where(key <= tok, s, NEG_INF)
            accumulate(b, s, lambda h, vn=vn: _with_ones(vn[:, h * LANES:(h + 1) * LANES]))
            o = acc_scr[b, :, :LANES] / acc_scr[b, :, LANES:]
            for h in range(n_kv):
                for g in range(2):
                    r1 = ((h * 2 + 0) * 2 + g) * t
                    r2 = ((h * 2 + 1) * 2 + g) * t
                    o_ref[b * t:(b + 1) * t, (2 * h + g) * LANES:(2 * h + g + 1) * LANES] = (
                        _sub_norm(o[r1:r1 + t, :], o[r2:r2 + t, :], lam, gsub_ref[...], lam_init))


def _a_sample_attention(q, k, v, cache_kt, cache_v, layer, page_table, lam_vecs, g_sub,
                        lam_init, dec_b, dec_t):
    n_pages = page_table.shape[1]
    kw, page = cache_kt.shape[2], cache_kt.shape[3]
    n_kv = kw // LANES
    assert 4 * dec_t * n_kv == LANES and page == LANES
    assert cache_v.shape[2:] == (page * n_kv, LANES)
    n_seq = 2 if dec_b % 2 == 0 else 1
    n_pages_step = math.gcd(n_pages, 16)
    rows = n_seq * dec_t
    vec = lambda a: a.reshape(1, -1)

    def page_spec(b, r):
        return pl.BlockSpec(
            (None, None, kw, page),
            lambda i, j, pt: (layer, pt[i * n_seq + b, j * n_pages_step + r], 0, 0))

    pages = [page_spec(b, r) for b in range(n_seq) for r in range(n_pages_step)]
    tok_map = lambda i, j, pt: (i, 0)
    grid_spec = pltpu.PrefetchScalarGridSpec(
        num_scalar_prefetch=1,
        grid=(dec_b // n_seq, n_pages // n_pages_step),
        in_specs=[pl.BlockSpec((rows, q.shape[1]), tok_map),
                  pl.BlockSpec((rows, kw), tok_map),
                  pl.BlockSpec((rows, kw), tok_map)]
                 + [pl.BlockSpec((1, HEAD_DIM), lambda i, j, pt: (0, 0))] * 4
                 + [pl.BlockSpec((1, LANES), lambda i, j, pt: (0, 0))]
                 + pages * 2,
        out_specs=pl.BlockSpec((rows, q.shape[1]), tok_map),
        scratch_shapes=[pltpu.VMEM((n_seq, LANES, kw), F32),
                        pltpu.VMEM((n_seq, kw, n_pages_step * page), BF16),
                        pltpu.VMEM((n_seq, n_kv, n_pages_step * page, 2 * LANES), BF16),
                        pltpu.VMEM((n_seq, LANES, LANES), F32),
                        pltpu.VMEM((n_seq, LANES, 2 * LANES), F32)])
    n_win = n_seq * n_pages_step
    return pl.pallas_call(
        functools.partial(_asample_kernel, n_seq=n_seq, n_pages_step=n_pages_step, n_kv=n_kv,
                          page=page, lam_init=lam_init),
        grid_spec=grid_spec,
        out_shape=jax.ShapeDtypeStruct((dec_b * dec_t, q.shape[1]), F32),
        compiler_params=_params(2),
        name="a_sample_attention",
    )(page_table, q, k, v, *[vec(a) for a in lam_vecs], vec(g_sub),
      *([cache_kt] * n_win), *([cache_v] * n_win))


def _dup_head(x, h):
    lane = lax.broadcasted_iota(jnp.int32, x.shape, 1)
    keep = (lane < HEAD_DIM) if h == 0 else (lane >= HEAD_DIM)
    return jnp.where(keep, x, pltpu.roll(x, HEAD_DIM, 1))


def _swa_heads(q_ref, k2, v2, sink_ref, o_ref, valid, t, n_group):
    lane = lax.broadcasted_iota(jnp.int32, (t, LANES), 1)
    first = lane < HEAD_DIM
    n_kv = k2.shape[1] // HEAD_DIM
    pairs = n_group // 2
    for h in range(n_kv):
        kd = _dup_head(k2, h).astype(BF16)
        vd = _dup_head(v2, h).astype(BF16)
        parts = []
        for r in range(n_group):
            c = h * pairs + r // 2
            q2 = q_ref[:, c * LANES:(c + 1) * LANES]
            parts.append(jnp.where(first if r % 2 == 0 else jnp.logical_not(first), q2, 0.0))
        qs = jnp.concatenate(parts, axis=0).astype(BF16)
        s = _dot_nt(qs, kd)
        ps = []
        for r in range(n_group):
            sr = jnp.where(valid, s[r * t:(r + 1) * t, :], NEG_INF)
            sink = sink_ref[h * n_group + r]
            m = jnp.maximum(jnp.max(sr, axis=1, keepdims=True), sink)
            p = jnp.exp(sr - m)
            l = jnp.sum(p, axis=1, keepdims=True) + jnp.exp(sink - m)
            ps.append(p / l)
        o = _dot(jnp.concatenate(ps, axis=0).astype(BF16), vd)
        for pr in range(pairs):
            c = h * pairs + pr
            o_ref[:, c * LANES:(c + 1) * LANES] = jnp.where(
                first, o[(2 * pr) * t:(2 * pr + 1) * t, :], o[(2 * pr + 1) * t:(2 * pr + 2) * t, :])


def _bprompt_kernel(q_ref, kp_ref, kc_ref, vp_ref, vc_ref, sink_ref, o_ref, *, n_group):
    i = pl.program_id(1)
    t = q_ref.shape[0]
    k2 = jnp.concatenate([kp_ref[...], kc_ref[...]], axis=0)
    v2 = jnp.concatenate([vp_ref[...], vc_ref[...]], axis=0)
    tok = lax.broadcasted_iota(jnp.int32, (t, 2 * t), 0)
    key = lax.broadcasted_iota(jnp.int32, (t, 2 * t), 1)
    valid = (key > tok) & (key <= tok + t) & (key >= jnp.where(i > 0, 0, t))
    _swa_heads(q_ref, k2, v2, sink_ref, o_ref, valid, t, n_group)


def _b_prompt_attention(q, kb, vb, sinks, batch, seq):
    t = WINDOW
    nb = seq // t
    n_group = (q.shape[1] // HEAD_DIM) // (kb.shape[1] // HEAD_DIM)
    cur = lambda b, i: (b * nb + i, 0)
    prev = lambda b, i: (b * nb + jnp.maximum(i - 1, 0), 0)
    return pl.pallas_call(
        functools.partial(_bprompt_kernel, n_group=n_group),
        grid=(batch, nb),
        in_specs=[pl.BlockSpec((t, q.shape[1]), cur),
                  pl.BlockSpec((t, LANES), prev), pl.BlockSpec((t, LANES), cur),
                  pl.BlockSpec((t, LANES), prev), pl.BlockSpec((t, LANES), cur),
                  pl.BlockSpec(memory_space=pltpu.SMEM)],
        out_specs=pl.BlockSpec((t, q.shape[1]), cur),
        out_shape=jax.ShapeDtypeStruct((batch * seq, q.shape[1]), F32),
        compiler_params=_params(2),
        name="b_prompt_attention",
    )(q, kb, kb, vb, vb, sinks)


def _bsample_kernel(q_ref, kw_ref, kn_ref, vw_ref, vn_ref, sink_ref, o_ref, *, n_group):
    t = q_ref.shape[0]
    w = kw_ref.shape[0]
    pad = jnp.zeros((w - t, LANES), F32)
    k2 = jnp.concatenate([kw_ref[...], kn_ref[...], pad], axis=0)
    v2 = jnp.concatenate([vw_ref[...], vn_ref[...], pad], axis=0)
    tok = lax.broadcasted_iota(jnp.int32, (t, 2 * w), 0)
    key = lax.broadcasted_iota(jnp.int32, (t, 2 * w), 1)
    valid = (key > tok + (w - WINDOW)) & (key <= tok + w)
    _swa_heads(q_ref, k2, v2, sink_ref, o_ref, valid, t, n_group)


def _b_sample_attention(q, kb, vb, cache_k_win, cache_v_win, sinks, n_prompt, dec_b, dec_t):
    w = cache_k_win.shape[1]
    row0 = n_prompt // dec_t
    n_group = (q.shape[1] // HEAD_DIM) // (kb.shape[1] // HEAD_DIM)
    tok = lambda b: (row0 + b, 0)
    win = pl.BlockSpec((None, w, LANES), lambda b: (b, 0, 0))
    return pl.pallas_call(
        functools.partial(_bsample_kernel, n_group=n_group),
        grid=(dec_b,),
        in_specs=[pl.BlockSpec((dec_t, q.shape[1]), tok),
                  win, pl.BlockSpec((dec_t, LANES), tok),
                  win, pl.BlockSpec((dec_t, LANES), tok),
                  pl.BlockSpec(memory_space=pltpu.SMEM)],
        out_specs=pl.BlockSpec((dec_t, q.shape[1]), lambda b: (b, 0)),
        out_shape=jax.ShapeDtypeStruct((dec_b * dec_t, q.shape[1]), F32),
        compiler_params=_params(1),
        name="b_sample_attention",
    )(q, cache_k_win.reshape(dec_b, w, LANES), kb, cache_v_win.reshape(dec_b, w, LANES), vb,
      sinks)


def _token_tile(seq, n_sample):
    for tm in (512, 256, 128, 64, 32, 16, 8):
        if seq % tm == 0 and n_sample % tm == 0:
            return tm
    raise ValueError("unsupported token counts")


def kernel(x_prompt, x_sample, cache_k_a, cache_v_a, cache_k_win, cache_v_win, page_table, p_prompt, p_sample, g_ffn1, w_ffn1_in, w_ffn1_out, g_mix, g_ffn2, w_ffn2_in, w_ffn2_out, w_qkv_a, g_q_a, g_k_a, lambda_q1, lambda_k1, lambda_q2, lambda_k2, g_sub_a, w_o_a, g_kv, w_kv, g_k_b, w_q_b, g_q_b, sinks_b, w_o_b, w_ple_up, g_ple_post, g_ple_gate, w_ple_gate):
    batch, seq, d = x_prompt.shape
    dec_b, dec_t, _ = x_sample.shape
    depth = g_ffn1.shape[0]
    n_a = w_qkv_a.shape[0]
    n_pool, page = cache_k_a.shape[1], cache_k_a.shape[2]
    a_kv = cache_k_a.shape[3]
    past_len = page_table.shape[1] * page
    w_buf = cache_k_win.shape[1]
    n_prompt = batch * seq
    n_sample = dec_b * dec_t
    tm = _token_tile(seq, n_sample)
    assert tm % dec_t == 0 and seq % WINDOW == 0 and w_buf == WINDOW

    bf16_stacks = {}

    def bf(w_stack, layer):
        if id(w_stack) not in bf16_stacks:
            bf16_stacks[id(w_stack)] = w_stack.astype(BF16)
        return _LayerWeight(bf16_stacks[id(w_stack)], layer)

    p_prompt = p_prompt.reshape(depth, n_prompt, -1)
    p_sample = p_sample.reshape(depth, n_sample, -1)
    cache_kt = jnp.transpose(cache_k_a, (0, 1, 3, 4, 5, 2)).reshape(n_a, n_pool, -1, page)
    cache_v = cache_v_a.reshape(n_a, n_pool, page * a_kv, -1)

    n_pt, spt, n_st = n_prompt // tm, seq // tm, n_sample // tm
    rope = _rope_tables(seq, dec_t, past_len, tm)
    prompt_tab = lambda i: (i % spt, 0)
    sample_tab = lambda i: (spt, 0)
    all_tab = lambda i: (jnp.where(i < n_pt, i % spt, spt), 0)
    attn_blk = next(b for b in (512, 256, WINDOW) if seq % b == 0)
    kv_chunks = a_kv * 2 * HEAD_DIM // LANES
    a_segments = [(2 * kv_chunks, 0, QK_SCALE), (kv_chunks, 1, 1.0), (kv_chunks, None, 1.0)]

    ka_p, va_p, ka_s, va_s = [], [], [], []
    kb = vb = None
    for i in range(depth):
        if i == n_a:
            kb, vb = _project(x, g_kv, bf(w_kv[None], 0), [g_k_b],
                              [(1, 0, 1.0), (1, None, 1.0)],
                              rope, all_tab, 0, n_pt + n_st, tm, "shared_kv_proj")
        if i == 0:
            x = _ffn_half_join(x_prompt.reshape(n_prompt, d), x_sample.reshape(n_sample, d),
                               g_ffn1[i], bf(w_ffn1_in, i), bf(w_ffn1_out, i), tm)
        else:
            x = _ffn_half(x, g_ffn1[i], bf(w_ffn1_in, i), bf(w_ffn1_out, i), tm)
        if i < n_a:
            lam_init = 0.8 - 0.6 * math.exp(-0.3 * i)
            lam_vecs = (lambda_q1[i], lambda_k1[i], lambda_q2[i], lambda_k2[i])
            w_qkv, gains = bf(w_qkv_a, i), [g_q_a[i], g_k_a[i]]
            q, k, v, k_cache, v_cache = _project(
                x, g_mix[i], w_qkv, gains, a_segments, rope, prompt_tab, 0, n_pt, tm,
                "a_qkv_proj_prompt", cache_layout=(batch, seq))
            q_s, k_s, v_s = _project(
                x, g_mix[i], w_qkv, gains, a_segments, rope, sample_tab, n_pt, n_st, tm,
                "a_qkv_proj_sample")
            ka_p.append(k_cache)
            va_p.append(v_cache)
            ka_s.append(k_s)
            va_s.append(v_s)
            o_p = _a_prompt_attention(q, k, v, lam_vecs, g_sub_a[i], lam_init, batch, seq,
                                      attn_blk)
            o_s = _a_sample_attention(q_s, k_s, v_s, cache_kt, cache_v, i, page_table,
                                      lam_vecs, g_sub_a[i], lam_init, dec_b, dec_t)
            w_o = bf(w_o_a, i)
        else:
            j = i - n_a
            (q,) = _project(x, g_mix[i], bf(w_q_b, j), [g_q_b[j]],
                            [(w_q_b.shape[2] // LANES, 0, QK_SCALE)], rope, all_tab, 0,
                            n_pt + n_st, tm, "b_q_proj")
            o_p = _b_prompt_attention(q, kb, vb, sinks_b[j], batch, seq)
            o_s = _b_sample_attention(q, kb, vb, cache_k_win, cache_v_win, sinks_b[j],
                                      n_prompt, dec_b, dec_t)
            w_o = bf(w_o_b, j)
        ys = _post_mixer(x, o_p, o_s, w_o, g_ffn2[i], bf(w_ffn2_in, i), bf(w_ffn2_out, i),
                         p_prompt, p_sample, i, bf(w_ple_up, i), g_ple_post[i], g_ple_gate[i],
                         bf(w_ple_gate, i), tm, split=(i == depth - 1))
        x = ys[0]
    y_prompt, y_sample = ys

    ka_p = jnp.transpose(jnp.stack(ka_p).reshape(n_a, batch, a_kv, 2, HEAD_DIM, seq),
                         (0, 1, 5, 2, 3, 4))
    va_p = jnp.stack(va_p).reshape(n_a, batch, seq, a_kv, 2 * HEAD_DIM)
    ka_s = jnp.stack(ka_s).reshape(n_a, dec_b, dec_t, a_kv, 2, HEAD_DIM)
    va_s = jnp.stack(va_s).reshape(n_a, dec_b, dec_t, a_kv, 2 * HEAD_DIM)
    b_kv = kb.shape[1] // HEAD_DIM
    kb_p = kb[:n_prompt].reshape(batch, seq, b_kv, HEAD_DIM)
    vb_p = vb[:n_prompt].reshape(batch, seq, b_kv, HEAD_DIM)
    kb_s = kb[n_prompt:].reshape(dec_b, dec_t, b_kv, HEAD_DIM)
    vb_s = vb[n_prompt:].reshape(dec_b, dec_t, b_kv, HEAD_DIM)
    kw_s = jnp.concatenate([cache_k_win, kb_s], axis=1)
    vw_s = jnp.concatenate([cache_v_win, vb_s], axis=1)
    return (y_prompt.reshape(batch, seq, d), y_sample.reshape(dec_b, dec_t, d),
            ka_p, va_p, ka_s, va_s,
            kb_p[:, -w_buf:], vb_p[:, -w_buf:], kw_s[:, -w_buf:], vw_s[:, -w_buf:])
```
